```python
import jax, jax.numpy as jnp
from jax import lax
import numpy as np

D_MODEL = 1024
BATCH = 4
SEQ = 4096
DEPTH = 1

POOL_WINDOWS = (2, 4, 8, 16)
N_POOL_GROUPS = len(POOL_WINDOWS)
D_POOL = D_MODEL // 2
POOL_GROUP = D_POOL // N_POOL_GROUPS
D_RNN = D_MODEL
N_RNN_HEADS = 8
RNN_HEAD = D_RNN // N_RNN_HEADS
CONV_WIDTH = 4
LRU_C = 8.0
N_BRANCHES = 2
D_IN = D_POOL + 2 * D_RNN + N_BRANCHES * D_MODEL
D_FF = -(-8 * D_MODEL // (3 * 256)) * 256
NORM_EPS = 1e-6

kernel_name = "hybrid_pool_rglru_gated_block"


def rmsnorm(x, g):
    xf = x.astype(jnp.float32)
    y = xf * lax.rsqrt(jnp.mean(xf * xf, axis=-1, keepdims=True) + NORM_EPS)
    return (y * g.astype(jnp.float32)).astype(x.dtype)


def pool_mixer(u, w_grp, scale):
    B, S, _ = u.shape
    uf = u.astype(jnp.float32).reshape(B, S, N_POOL_GROUPS, POOL_GROUP)
    c = jnp.cumsum(uf, axis=1)
    pos = jnp.arange(S)
    outs = []
    for g, w in enumerate(POOL_WINDOWS):
        cg = c[:, :, g]
        c_lo = jnp.pad(cg[:, : S - w], ((0, 0), (w, 0), (0, 0)))
        count = jnp.minimum(pos + 1, w).astype(jnp.float32)[None, :, None]
        outs.append((cg - c_lo) / count - uf[:, :, g])
    pooled = jnp.stack(outs, axis=2).astype(u.dtype)
    mixed = jnp.einsum("bsgc,gcd->bsgd", pooled, w_grp)
    return mixed.reshape(B, S, D_POOL) * scale


def causal_depthwise_conv(u, w, b):
    S = u.shape[1]
    up = jnp.pad(u, ((0, 0), (CONV_WIDTH - 1, 0), (0, 0)))
    y = b
    for k in range(CONV_WIDTH):
        y = y + up[:, k : k + S] * w[k]
    return y


def rg_lru(v, w_a, b_a, w_x, b_x, lam):
    B, S, _ = v.shape
    vh = v.reshape(B, S, N_RNN_HEADS, RNN_HEAD)
    r = jax.nn.sigmoid((jnp.einsum("bshi,hij->bshj", vh, w_a) + b_a).astype(jnp.float32)).reshape(B, S, D_RNN)
    i = jax.nn.sigmoid((jnp.einsum("bshi,hij->bshj", vh, w_x) + b_x).astype(jnp.float32)).reshape(B, S, D_RNN)
    log_a = -LRU_C * r * jax.nn.softplus(-lam.astype(jnp.float32))
    a = jnp.exp(log_a)
    b = jnp.sqrt(-jnp.expm1(2.0 * log_a)) * i * v.astype(jnp.float32)

    def combine(left, right):
        a1, b1 = left
        a2, b2 = right
        return a1 * a2, a2 * b1 + b2

    _, h = lax.associative_scan(combine, (a, b), axis=1)
    return h.astype(v.dtype)


def setup_inputs(seed: int = 0) -> dict:
    key = jax.random.key(seed)
    ks = jax.random.split(key, 22)
    f32 = jnp.float32
    L = DEPTH

    def nrm(k, shape, fan_in):
        return jax.random.normal(k, shape, f32) * fan_in ** -0.5

    def small(k, shape, s=0.02):
        return jax.random.normal(k, shape, f32) * s

    x = jax.random.normal(ks[0], (BATCH, SEQ, D_MODEL), f32)
    norm_mix = 1.0 + small(ks[1], (L, D_MODEL))
    w_in = nrm(ks[2], (L, D_MODEL, D_IN), D_MODEL)
    w_pool_grp = nrm(ks[3], (L, N_POOL_GROUPS, POOL_GROUP, POOL_GROUP), POOL_GROUP)
    pool_scale = 1.0 + small(ks[4], (L, D_POOL))
    w_pool_out = nrm(ks[5], (L, D_POOL, D_MODEL), D_POOL)
    conv_w = nrm(ks[6], (L, CONV_WIDTH, D_RNN), CONV_WIDTH)
    conv_b = small(ks[7], (L, D_RNN))
    w_rg_a = nrm(ks[8], (L, N_RNN_HEADS, RNN_HEAD, RNN_HEAD), RNN_HEAD)
    b_rg_a = small(ks[9], (L, N_RNN_HEADS, RNN_HEAD))
    w_rg_x = nrm(ks[10], (L, N_RNN_HEADS, RNN_HEAD, RNN_HEAD), RNN_HEAD)
    b_rg_x = small(ks[11], (L, N_RNN_HEADS, RNN_HEAD))
    a_c = jax.random.uniform(ks[12], (L, D_RNN), f32, minval=0.9, maxval=0.999)
    a0 = a_c ** (1.0 / LRU_C)
    lru_lambda = jnp.log(a0) - jnp.log1p(-a0)
    w_rnn_out = nrm(ks[13], (L, D_RNN, D_MODEL), D_RNN)
    w_o = nrm(ks[14], (L, D_MODEL, D_MODEL), D_MODEL)
    norm_ffn = 1.0 + small(ks[15], (L, D_MODEL))
    w_ffn_in = nrm(ks[16], (L, D_MODEL, 2 * D_FF), D_MODEL)
    w_ffn_out = nrm(ks[17], (L, D_FF, D_MODEL), D_FF)
    norm_final = 1.0 + small(ks[18], (D_MODEL,))
    return {"x": x, "norm_mix": norm_mix, "w_in": w_in, "w_pool_grp": w_pool_grp,
            "pool_scale": pool_scale, "w_pool_out": w_pool_out, "conv_w": conv_w, "conv_b": conv_b,
            "w_rg_a": w_rg_a, "b_rg_a": b_rg_a, "w_rg_x": w_rg_x, "b_rg_x": b_rg_x,
            "lru_lambda": lru_lambda, "w_rnn_out": w_rnn_out, "w_o": w_o, "norm_ffn": norm_ffn,
            "w_ffn_in": w_ffn_in, "w_ffn_out": w_ffn_out, "norm_final": norm_final}


def reference(x, norm_mix, w_in, w_pool_grp, pool_scale, w_pool_out, conv_w, conv_b,
              w_rg_a, b_rg_a, w_rg_x, b_rg_x, lru_lambda, w_rnn_out, w_o, norm_ffn,
              w_ffn_in, w_ffn_out, norm_final):
    B, S, _ = x.shape
    for l in range(DEPTH):
        h = rmsnorm(x, norm_mix[l])
        proj = h @ w_in[l]
        o1 = D_POOL
        o2 = o1 + D_RNN
        o3 = o2 + D_RNN
        u_pool = proj[..., :o1]
        u_rnn = proj[..., o1:o2]
        u_gate = proj[..., o2:o3]
        g_merge = jax.nn.sigmoid(proj[..., o3:].reshape(B, S, N_BRANCHES, D_MODEL))

        y_pool = pool_mixer(u_pool, w_pool_grp[l], pool_scale[l]) @ w_pool_out[l]

        v = causal_depthwise_conv(u_rnn, conv_w[l], conv_b[l])
        hr = rg_lru(v, w_rg_a[l], b_rg_a[l], w_rg_x[l], b_rg_x[l], lru_lambda[l])
        y_rnn = (hr * jax.nn.gelu(u_gate)) @ w_rnn_out[l]

        mix = g_merge[:, :, 0] * y_pool + g_merge[:, :, 1] * y_rnn
        x = x + mix @ w_o[l]

        h = rmsnorm(x, norm_ffn[l])
        gu = h @ w_ffn_in[l]
        gate, up = gu[..., :D_FF], gu[..., D_FF:]
        x = x + (jax.nn.silu(gate) * up) @ w_ffn_out[l]
    return rmsnorm(x, norm_final)
```

```python
import functools
import math

import jax
import jax.numpy as jnp
from jax.experimental import pallas as pl
from jax.experimental.pallas import tpu as pltpu

D_MODEL = 1024
POOL_WINDOWS = (2, 4, 8, 16)
LANE = 128
SUBLANE = 8
D_POOL = LANE * len(POOL_WINDOWS)
D_RNN = D_MODEL
N_HEADS = D_RNN // LANE
CONV_WIDTH = 4
LRU_C = 8.0
D_FF = 2816
NORM_EPS = 1e-6

SEQ_TILE = 256
POOL_HALO = 16
CONV_HALO = 8
FF_CHUNK = 512
VMEM_LIMIT_BYTES = 58 * 1024 * 1024

_F32 = jnp.float32
_BF16 = jnp.bfloat16


def _dot(a, b):
    return jnp.dot(a, b, preferred_element_type=_F32)


def _rmsnorm(x, g):
    return x * jax.lax.rsqrt(jnp.mean(x * x, axis=-1, keepdims=True) + NORM_EPS) * g


def _sigmoid(x):
    return 0.5 * jnp.tanh(0.5 * x) + 0.5


def _gelu_tanh(x):
    c = math.sqrt(2.0 / math.pi)
    return 0.5 * x * (1.0 + jnp.tanh(c * (x + 0.044715 * (x * x * x))))


def _layer_kernel(x_ref, g_mix_ref, w_in_ref, w_grp_ref, pool_scale_ref, w_pool_out_ref,
                  conv_w_ref, conv_b_ref, w_ax_ref, b_ax_ref, lam_ref, w_rnn_out_ref, w_o_ref,
                  g_ffn_ref, w_ffn_in_ref, w_ffn_out_ref, g_final_ref,
                  o_ref,
                  pool_ext, rnn_ext, gate_buf, a_buf, b_buf, h_carry, hg_buf, act_buf):
    T = SEQ_TILE
    G = T // SUBLANE
    s = pl.program_id(1)

    @pl.when(s == 0)
    def _():
        pool_ext[0:POOL_HALO, :] = jnp.zeros((POOL_HALO, D_POOL), _F32)
        rnn_ext[0:CONV_HALO, :] = jnp.zeros((CONV_HALO, D_RNN), _F32)
        h_carry[...] = jnp.zeros_like(h_carry)

    x = x_ref[0]
    h = _rmsnorm(x, g_mix_ref[...]).astype(_BF16)

    o1 = D_POOL
    o2 = o1 + D_RNN
    o3 = o2 + D_RNN
    o4 = o3 + D_MODEL

    pool_ext[POOL_HALO:POOL_HALO + T, :] = _dot(h, w_in_ref[:, 0:o1])
    pos1 = s * T + 1 + jax.lax.broadcasted_iota(jnp.int32, (T, LANE), 0)
    mixed = []
    for g, w in enumerate(POOL_WINDOWS):
        cols = slice(g * LANE, (g + 1) * LANE)
        e = pool_ext[:, cols]
        win = e
        d = 1
        while d < w:
            win = win + pltpu.roll(win, d, axis=0)
            d *= 2
        count = jnp.minimum(pos1, w).astype(_F32)
        pooled = win[POOL_HALO:, :] / count - e[POOL_HALO:, :]
        mixed.append(_dot(pooled.astype(_BF16), w_grp_ref[g]) * pool_scale_ref[:, cols])
    pool_ext[0:POOL_HALO, :] = pool_ext[T:T + POOL_HALO, :]
    y_pool = _dot(jnp.concatenate(mixed, axis=1).astype(_BF16), w_pool_out_ref[...])

    rnn_ext[CONV_HALO:CONV_HALO + T, :] = _dot(h, w_in_ref[:, o1:o2])
    gate_buf[...] = _dot(h, w_in_ref[:, o2:o3])

    lam = lam_ref[...]
    softplus_neg_lam = jnp.maximum(-lam, 0.0) + jnp.log1p(jnp.exp(-jnp.abs(lam)))
    log_a_scale = -LRU_C * softplus_neg_lam

    row_g = jax.lax.broadcasted_iota(jnp.int32, (G, LANE), 0)
    for hd in range(N_HEADS):
        cols = slice(hd * LANE, (hd + 1) * LANE)
        e = rnn_ext[:, cols]
        v = conv_b_ref[:, cols]
        for k in range(CONV_WIDTH):
            shift = CONV_WIDTH - 1 - k
            tap = e if shift == 0 else pltpu.roll(e, shift, axis=0)
            v = v + tap[CONV_HALO:, :] * conv_w_ref[k:k + 1, cols]
        ri = _sigmoid(_dot(v.astype(_BF16), w_ax_ref[hd]) + b_ax_ref[hd:hd + 1, :])
        r = ri[:, 0:LANE]
        i = ri[:, LANE:2 * LANE]
        log_a = r * log_a_scale[:, cols]
        a = jnp.exp(log_a)
        b = jnp.sqrt(-jnp.tanh(log_a) * (1.0 + a * a)) * i * v
        a_buf[hd] = a
        b_buf[hd] = b

        def rows(k):
            return pl.ds(k, G, stride=SUBLANE)
        a_cum = a_buf[hd, rows(0), :]
        b_cum = b_buf[hd, rows(0), :]
        for k in range(1, SUBLANE):
            a_k = a_buf[hd, rows(k), :]
            b_cum = a_k * b_cum + b_buf[hd, rows(k), :]
            a_cum = a_k * a_cum
            a_buf[hd, rows(k), :] = a_cum
            b_buf[hd, rows(k), :] = b_cum
        d = 1
        while d < G:
            a_sh = jnp.where(row_g < d, 1.0, pltpu.roll(a_cum, d, axis=0))
            b_sh = jnp.where(row_g < d, 0.0, pltpu.roll(b_cum, d, axis=0))
            b_cum = a_cum * b_sh + b_cum
            a_cum = a_cum * a_sh
            d *= 2
        h0 = h_carry[hd:hd + 1, :]
        h_out = a_cum * h0 + b_cum
        h_carry[hd:hd + 1, :] = h_out[G - 1:G, :]
        h_in = jnp.where(row_g == 0, h0, pltpu.roll(h_out, 1, axis=0))
        for k in range(SUBLANE):
            b_buf[hd, rows(k), :] = b_buf[hd, rows(k), :] + a_buf[hd, rows(k), :] * h_in
        hg_buf[:, cols] = (b_buf[hd] * _gelu_tanh(gate_buf[:, cols])).astype(_BF16)
    rnn_ext[0:CONV_HALO, :] = rnn_ext[T:T + CONV_HALO, :]
    y_rnn = _dot(hg_buf[...], w_rnn_out_ref[...])

    mix = (_sigmoid(_dot(h, w_in_ref[:, o3:o4])) * y_pool
           + _sigmoid(_dot(h, w_in_ref[:, o4:o4 + D_MODEL])) * y_rnn)
    x1 = x + _dot(mix.astype(_BF16), w_o_ref[...])

    h2 = _rmsnorm(x1, g_ffn_ref[...]).astype(_BF16)
    for c in range(0, D_FF, FF_CHUNK):
        wd = min(FF_CHUNK, D_FF - c)
        gate = _dot(h2, w_ffn_in_ref[:, c:c + wd])
        up = _dot(h2, w_ffn_in_ref[:, D_FF + c:D_FF + c + wd])
        act_buf[:, c:c + wd] = (gate * _sigmoid(gate) * up).astype(_BF16)
    x2 = x1 + _dot(act_buf[...], w_ffn_out_ref[...])

    o_ref[0] = _rmsnorm(x2, g_final_ref[...])


def _resident(shape):
    n = len(shape)
    return pl.BlockSpec(shape, lambda b, s: (0,) * n, pipeline_mode=pl.Buffered(1))


@jax.jit
def kernel(x, norm_mix, w_in, w_pool_grp, pool_scale, w_pool_out, conv_w, conv_b, w_rg_a, b_rg_a,
           w_rg_x, b_rg_x, lru_lambda, w_rnn_out, w_o, norm_ffn, w_ffn_in, w_ffn_out, norm_final):
    B, S, D = x.shape
    assert D == D_MODEL and S % SEQ_TILE == 0
    assert norm_mix.shape[0] == 1, "single-layer stack"
    T = SEQ_TILE

    w_ax = jnp.concatenate([w_rg_a[0], w_rg_x[0]], axis=-1).astype(_BF16)
    b_ax = jnp.concatenate([b_rg_a[0], b_rg_x[0]], axis=-1)
    operands = [
        x,
        norm_mix[0].reshape(1, D),
        w_in[0].astype(_BF16),
        w_pool_grp[0].astype(_BF16),
        pool_scale[0].reshape(1, D_POOL),
        w_pool_out[0].astype(_BF16),
        conv_w[0],
        conv_b[0].reshape(1, D_RNN),
        w_ax,
        b_ax,
        lru_lambda[0].reshape(1, D_RNN),
        w_rnn_out[0].astype(_BF16),
        w_o[0].astype(_BF16),
        norm_ffn[0].reshape(1, D),
        w_ffn_in[0].astype(_BF16),
        w_ffn_out[0].astype(_BF16),
        norm_final.reshape(1, D),
    ]
    in_specs = [pl.BlockSpec((1, T, D), lambda b, s: (b, s, 0))]
    in_specs += [_resident(op.shape) for op in operands[1:]]

    return pl.pallas_call(
        _layer_kernel,
        grid=(B, S // T),
        in_specs=in_specs,
        out_specs=pl.BlockSpec((1, T, D), lambda b, s: (b, s, 0)),
        out_shape=jax.ShapeDtypeStruct((B, S, D), x.dtype),
        scratch_shapes=[
            pltpu.VMEM((POOL_HALO + T, D_POOL), _F32),
            pltpu.VMEM((CONV_HALO + T, D_RNN), _F32),
            pltpu.VMEM((T, D_RNN), _F32),
            pltpu.VMEM((N_HEADS, T, LANE), _F32),
            pltpu.VMEM((N_HEADS, T, LANE), _F32),
            pltpu.VMEM((N_HEADS, LANE), _F32),
            pltpu.VMEM((T, D_RNN), _BF16),
            pltpu.VMEM((T, D_FF), _BF16),
        ],
        compiler_params=pltpu.CompilerParams(
            dimension_semantics=("arbitrary", "arbitrary"),
            vmem_limit_bytes=VMEM_LIMIT_BYTES,
        ),
        name="hybrid_layer",
    )(*operands)
```

```python
import functools
import math

import jax
import jax.numpy as jnp
from jax.experimental import pallas as pl
from jax.experimental.pallas import tpu as pltpu

D_MODEL = 1024
POOL_WINDOWS = (2, 4, 8, 16)
LANE = 128
SUBLANE = 8
D_POOL = LANE * len(POOL_WINDOWS)
D_RNN = D_MODEL
N_HEADS = D_RNN // LANE
CONV_WIDTH = 4
LRU_C = 8.0
D_FF = 2816
NORM_EPS = 1e-6

SEQ_TILE = 256
POOL_HALO = 16
CONV_HALO = 8
FF_CHUNK = 512
FF_CHUNK_STARTS = tuple(range(0, D_FF, FF_CHUNK))
VMEM_LIMIT_BYTES = 58 * 1024 * 1024

O_RNN = D_POOL
O_GATE = O_RNN + D_RNN
O_GPOOL = O_GATE + D_RNN
O_GRNN = O_GPOOL + D_MODEL

_F32 = jnp.float32
_BF16 = jnp.bfloat16


def _dot(a, b):
    return jnp.dot(a, b, preferred_element_type=_F32)


def _rmsnorm(x, g):
    return x * jax.lax.rsqrt(jnp.mean(x * x, axis=-1, keepdims=True) + NORM_EPS) * g


def _sigmoid(x):
    return 0.5 * jnp.tanh(0.5 * x) + 0.5


def _gelu_tanh(x):
    c = math.sqrt(2.0 / math.pi)
    return 0.5 * x * (1.0 + jnp.tanh(c * (x + 0.044715 * (x * x * x))))


def _layer_kernel(tiles_per_seq,
                  x_ref, g_mix_ref, w_in_ref, w_grp_ref, pool_scale_ref, w_pool_out_ref,
                  conv_w_ref, conv_b_ref, w_ax_ref, b_ax_ref, lam_ref, w_rnn_out_ref, w_o_ref,
                  g_ffn_ref, w_ffn_in_ref, w_ffn_out_ref, g_final_ref,
                  o_ref,
                  pool_ext, rnn_ext, gate_buf, a_buf, b_buf, h_carry, hg_buf, act_buf,
                  x1_buf, h_buf, h2_buf, sg_pool_buf, sg_rnn_buf):
    T = SEQ_TILE
    G = T // SUBLANE
    j = pl.program_id(0)
    s = j % tiles_per_seq

    @pl.when(j == 0)
    def _():
        x1_buf[...] = jnp.zeros_like(x1_buf)
        h2_buf[...] = jnp.zeros_like(h2_buf)

    @pl.when(s == 0)
    def _():
        pool_ext[0:POOL_HALO, :] = jnp.zeros((POOL_HALO, D_POOL), _F32)
        rnn_ext[0:CONV_HALO, :] = jnp.zeros((CONV_HALO, D_RNN), _F32)
        h_carry[...] = jnp.zeros_like(h_carry)

    def ffn_chunk(n):
        c = FF_CHUNK_STARTS[n]
        wd = min(FF_CHUNK, D_FF - c)
        gate = _dot(h2_buf[...], w_ffn_in_ref[:, c:c + wd])
        up = _dot(h2_buf[...], w_ffn_in_ref[:, D_FF + c:D_FF + c + wd])
        act_buf[:, c:c + wd] = (gate * _sigmoid(gate) * up).astype(_BF16)

    def ffn_out():
        x2 = x1_buf[...] + _dot(act_buf[...], w_ffn_out_ref[...])
        o_ref[0] = _rmsnorm(x2, g_final_ref[...])

    def pool_mixer():
        pool_ext[POOL_HALO:POOL_HALO + T, :] = _dot(h_buf[...], w_in_ref[:, 0:O_RNN])
        pos1 = s * T + 1 + jax.lax.broadcasted_iota(jnp.int32, (T, LANE), 0)
        mixed = []
        for g, w in enumerate(POOL_WINDOWS):
            cols = slice(g * LANE, (g + 1) * LANE)
            e = pool_ext[:, cols]
            win = e
            d = 1
            while d < w:
                win = win + pltpu.roll(win, d, axis=0)
                d *= 2
            count = jnp.minimum(pos1, w).astype(_F32)
            pooled = win[POOL_HALO:, :] / count - e[POOL_HALO:, :]
            mixed.append(_dot(pooled.astype(_BF16), w_grp_ref[g]) * pool_scale_ref[:, cols])
        pool_ext[0:POOL_HALO, :] = pool_ext[T:T + POOL_HALO, :]
        return _dot(jnp.concatenate(mixed, axis=1).astype(_BF16), w_pool_out_ref[...])

    lam = lam_ref[...]
    softplus_neg_lam = jnp.maximum(-lam, 0.0) + jnp.log1p(jnp.exp(-jnp.abs(lam)))
    log_a_scale = -LRU_C * softplus_neg_lam
    row_g = jax.lax.broadcasted_iota(jnp.int32, (G, LANE), 0)

    def rows(k):
        return pl.ds(k, G, stride=SUBLANE)

    def rnn_head(hd):
        cols = slice(hd * LANE, (hd + 1) * LANE)
        e = rnn_ext[:, cols]
        v = conv_b_ref[:, cols]
        for k in range(CONV_WIDTH):
            shift = CONV_WIDTH - 1 - k
            tap = e if shift == 0 else pltpu.roll(e, shift, axis=0)
            v = v + tap[CONV_HALO:, :] * conv_w_ref[k:k + 1, cols]
        ri = _sigmoid(_dot(v.astype(_BF16), w_ax_ref[hd]) + b_ax_ref[hd:hd + 1, :])
        r = ri[:, 0:LANE]
        i = ri[:, LANE:2 * LANE]
        log_a = r * log_a_scale[:, cols]
        a = jnp.exp(log_a)
        b = jnp.sqrt(-jnp.tanh(log_a) * (1.0 + a * a)) * i * v
        a_buf[hd] = a
        b_buf[hd] = b

        a_cum = a_buf[hd, rows(0), :]
        b_cum = b_buf[hd, rows(0), :]
        for k in range(1, SUBLANE):
            a_k = a_buf[hd, rows(k), :]
            b_cum = a_k * b_cum + b_buf[hd, rows(k), :]
            a_cum = a_k * a_cum
            a_buf[hd, rows(k), :] = a_cum
            b_buf[hd, rows(k), :] = b_cum
        d = 1
        while d < G:
            a_sh = jnp.where(row_g < d, 1.0, pltpu.roll(a_cum, d, axis=0))
            b_sh = jnp.where(row_g < d, 0.0, pltpu.roll(b_cum, d, axis=0))
            b_cum = a_cum * b_sh + b_cum
            a_cum = a_cum * a_sh
            d *= 2
        h0 = h_carry[hd:hd + 1, :]
        h_out = a_cum * h0 + b_cum
        h_carry[hd:hd + 1, :] = h_out[G - 1:G, :]
        h_in = jnp.where(row_g == 0, h0, pltpu.roll(h_out, 1, axis=0))
        for k in range(SUBLANE):
            b_buf[hd, rows(k), :] = b_buf[hd, rows(k), :] + a_buf[hd, rows(k), :] * h_in
        hg_buf[:, cols] = (b_buf[hd] * _gelu_tanh(gate_buf[:, cols])).astype(_BF16)

    assert len(FF_CHUNK_STARTS) == N_HEADS - 2
    ffn_chunk(0)
    h_buf[...] = _rmsnorm(x_ref[0], g_mix_ref[...]).astype(_BF16)
    y_pool = pool_mixer()
    rnn_ext[CONV_HALO:CONV_HALO + T, :] = _dot(h_buf[...], w_in_ref[:, O_RNN:O_GATE])
    gate_buf[...] = _dot(h_buf[...], w_in_ref[:, O_GATE:O_GPOOL])
    for hd in range(N_HEADS):
        rnn_head(hd)
        if hd + 1 < len(FF_CHUNK_STARTS):
            ffn_chunk(hd + 1)
        elif hd + 1 == len(FF_CHUNK_STARTS):
            sg_pool_buf[...] = _sigmoid(_dot(h_buf[...], w_in_ref[:, O_GPOOL:O_GRNN]))
        elif hd + 2 == N_HEADS:
            sg_rnn_buf[...] = _sigmoid(_dot(h_buf[...], w_in_ref[:, O_GRNN:O_GRNN + D_MODEL]))
        else:
            ffn_out()
    rnn_ext[0:CONV_HALO, :] = rnn_ext[T:T + CONV_HALO, :]
    y_rnn = _dot(hg_buf[...], w_rnn_out_ref[...])
    mix = sg_pool_buf[...] * y_pool + sg_rnn_buf[...] * y_rnn
    x1 = x_ref[0] + _dot(mix.astype(_BF16), w_o_ref[...])
    x1_buf[...] = x1
    h2_buf[...] = _rmsnorm(x1, g_ffn_ref[...]).astype(_BF16)


def _resident(shape):
    n = len(shape)
    return pl.BlockSpec(shape, lambda j: (0,) * n, pipeline_mode=pl.Buffered(1))


@jax.jit
def kernel(x, norm_mix, w_in, w_pool_grp, pool_scale, w_pool_out, conv_w, conv_b, w_rg_a, b_rg_a,
           w_rg_x, b_rg_x, lru_lambda, w_rnn_out, w_o, norm_ffn, w_ffn_in, w_ffn_out, norm_final):
    B, S, D = x.shape
    assert D == D_MODEL and S % SEQ_TILE == 0
    assert norm_mix.shape[0] == 1, "single-layer stack"
    T = SEQ_TILE
    nt = S // T
    n_tiles = B * nt

    w_ax = jnp.concatenate([w_rg_a[0], w_rg_x[0]], axis=-1).astype(_BF16)
    b_ax = jnp.concatenate([b_rg_a[0], b_rg_x[0]], axis=-1)
    operands = [
        x,
        norm_mix[0].reshape(1, D),
        w_in[0].astype(_BF16),
        w_pool_grp[0].astype(_BF16),
        pool_scale[0].reshape(1, D_POOL),
        w_pool_out[0].astype(_BF16),
        conv_w[0],
        conv_b[0].reshape(1, D_RNN),
        w_ax,
        b_ax,
        lru_lambda[0].reshape(1, D_RNN),
        w_rnn_out[0].astype(_BF16),
        w_o[0].astype(_BF16),
        norm_ffn[0].reshape(1, D),
        w_ffn_in[0].astype(_BF16),
        w_ffn_out[0].astype(_BF16),
        norm_final.reshape(1, D),
    ]

    def mixer_tile_index(j):
        m = jnp.minimum(j, n_tiles - 1)
        return (m // nt, m % nt, 0)

    def ffn_tile_index(j):
        m = jnp.maximum(j - 1, 0)
        return (m // nt, m % nt, 0)

    in_specs = [pl.BlockSpec((1, T, D), mixer_tile_index)]
    in_specs += [_resident(op.shape) for op in operands[1:]]

    return pl.pallas_call(
        functools.partial(_layer_kernel, nt),
        grid=(n_tiles + 1,),
        in_specs=in_specs,
        out_specs=pl.BlockSpec((1, T, D), ffn_tile_index),
        out_shape=jax.ShapeDtypeStruct((B, S, D), x.dtype),
        scratch_shapes=[
            pltpu.VMEM((POOL_HALO + T, D_POOL), _F32),
            pltpu.VMEM((CONV_HALO + T, D_RNN), _F32),
            pltpu.VMEM((T, D_RNN), _F32),
            pltpu.VMEM((N_HEADS, T, LANE), _F32),
            pltpu.VMEM((N_HEADS, T, LANE), _F32),
            pltpu.VMEM((N_HEADS, LANE), _F32),
            pltpu.VMEM((T, D_RNN), _BF16),
            pltpu.VMEM((T, D_FF), _BF16),
            pltpu.VMEM((T, D), _F32),
            pltpu.VMEM((T, D), _BF16),
            pltpu.VMEM((T, D), _BF16),
            pltpu.VMEM((T, D), _F32),
            pltpu.VMEM((T, D), _F32),
        ],
        compiler_params=pltpu.CompilerParams(
            dimension_semantics=("arbitrary",),
            vmem_limit_bytes=VMEM_LIMIT_BYTES,
        ),
        name="hybrid_layer",
    )(*operands)
```

```python
import functools
import math

import jax
import jax.numpy as jnp
from jax.experimental import pallas as pl
from jax.experimental.pallas import tpu as pltpu

D_MODEL = 1024
POOL_WINDOWS = (2, 4, 8, 16)
LANE = 128
SUBLANE = 8
D_POOL = LANE * len(POOL_WINDOWS)
D_RNN = D_MODEL
N_HEADS = D_RNN // LANE
CONV_WIDTH = 4
LRU_C = 8.0
D_FF = 2816
NORM_EPS = 1e-6

SEQ_TILE = 256
POOL_HALO = 16
CONV_HALO = 8
FF_CHUNK = 512
FF_CHUNK_STARTS = tuple(range(0, D_FF, FF_CHUNK))
FF_OUT_SPLIT = 1536
VMEM_LIMIT_BYTES = 58 * 1024 * 1024

O_RNN = D_POOL
O_GATE = O_RNN + D_RNN
O_GPOOL = O_GATE + D_RNN
O_GRNN = O_GPOOL + D_MODEL

_F32 = jnp.float32
_BF16 = jnp.bfloat16


def _dot(a, b):
    return jnp.dot(a, b, preferred_element_type=_F32)


def _rmsnorm(x, g):
    return x * jax.lax.rsqrt(jnp.mean(x * x, axis=-1, keepdims=True) + NORM_EPS) * g


def _sigmoid(x):
    return 0.5 * jnp.tanh(0.5 * x) + 0.5


def _gelu_tanh(x):
    c = math.sqrt(2.0 / math.pi)
    return 0.5 * x * (1.0 + jnp.tanh(c * (x + 0.044715 * (x * x * x))))


def _sqrt_nonneg(x):
    return jnp.where(x > 0.0, x * jax.lax.rsqrt(x), 0.0)


def _layer_kernel(tiles_per_seq,
                  x_ref, g_mix_ref, w_in_ref, w_grp_ref, pool_scale_ref, w_pool_out_ref,
                  conv_w_ref, conv_b_ref, w_ax_ref, b_ax_ref, lam_ref, w_rnn_out_ref, w_o_ref,
                  g_ffn_ref, w_ffn_in_ref, w_ffn_out_ref, g_final_ref,
                  o_ref,
                  pool_ext, rnn_ext, gate_buf, v_buf, vb_buf, a_buf, b_buf, hg_buf, act_buf,
                  x1_buf, h_buf, h2_buf, pooled_buf, mixed_buf, sg_pool_buf, sg_rnn_buf,
                  ffn_acc, h_carry):
    T = SEQ_TILE
    G = T // SUBLANE
    j = pl.program_id(0)
    s = j % tiles_per_seq

    @pl.when(j == 0)
    def _():
        x1_buf[...] = jnp.zeros_like(x1_buf)
        h2_buf[...] = jnp.zeros_like(h2_buf)

    @pl.when(s == 0)
    def _():
        pool_ext[0:POOL_HALO, :] = jnp.zeros((POOL_HALO, D_POOL), _F32)
        rnn_ext[0:CONV_HALO, :] = jnp.zeros((CONV_HALO, D_RNN), _F32)
        h_carry[...] = jnp.zeros_like(h_carry)

    def head_cols(hd):
        return slice(hd * LANE, (hd + 1) * LANE)

    def ffn_chunk(n):
        c = FF_CHUNK_STARTS[n]
        wd = min(FF_CHUNK, D_FF - c)
        gate = _dot(h2_buf[...], w_ffn_in_ref[:, c:c + wd])
        up = _dot(h2_buf[...], w_ffn_in_ref[:, D_FF + c:D_FF + c + wd])
        act_buf[:, c:c + wd] = (gate * _sigmoid(gate) * up).astype(_BF16)

    def ffn_out_first_half():
        ffn_acc[...] = x1_buf[...] + _dot(act_buf[:, 0:FF_OUT_SPLIT],
                                          w_ffn_out_ref[0:FF_OUT_SPLIT, :])

    def ffn_out_second_half():
        x2 = ffn_acc[...] + _dot(act_buf[:, FF_OUT_SPLIT:D_FF], w_ffn_out_ref[FF_OUT_SPLIT:D_FF, :])
        o_ref[0] = _rmsnorm(x2, g_final_ref[...])

    def pool_windows():
        pos1 = s * T + 1 + jax.lax.broadcasted_iota(jnp.int32, (POOL_HALO, LANE), 0)
        for g, w in enumerate(POOL_WINDOWS):
            cols = head_cols(g)
            e = pool_ext[:, cols]
            win = e
            d = 1
            while d < w:
                win = win + pltpu.roll(win, d, axis=0)
                d *= 2
            inv_head = 1.0 / jnp.minimum(pos1, w).astype(_F32)
            inv_count = jnp.concatenate(
                [inv_head, jnp.full((T - POOL_HALO, LANE), 1.0 / w, _F32)], axis=0)
            pooled = win[POOL_HALO:, :] * inv_count - e[POOL_HALO:, :]
            pooled_buf[:, cols] = pooled.astype(_BF16)
        pool_ext[0:POOL_HALO, :] = pool_ext[T:T + POOL_HALO, :]

    def pool_group_dots():
        for g in range(len(POOL_WINDOWS)):
            cols = head_cols(g)
            mixed = _dot(pooled_buf[:, cols], w_grp_ref[g]) * pool_scale_ref[:, cols]
            mixed_buf[:, cols] = mixed.astype(_BF16)

    def conv_all_heads():
        for hd in range(N_HEADS):
            cols = head_cols(hd)
            e = rnn_ext[:, cols]
            v = conv_b_ref[:, cols]
            for k in range(CONV_WIDTH):
                shift = CONV_WIDTH - 1 - k
                tap = e if shift == 0 else pltpu.roll(e, shift, axis=0)
                v = v + tap[CONV_HALO:, :] * conv_w_ref[k:k + 1, cols]
            v_buf[:, cols] = v
            vb_buf[:, cols] = v.astype(_BF16)
        rnn_ext[0:CONV_HALO, :] = rnn_ext[T:T + CONV_HALO, :]

    lam = lam_ref[...]
    softplus_neg_lam = jnp.maximum(-lam, 0.0) + jnp.log1p(jnp.exp(-jnp.abs(lam)))
    log_a_scale = -LRU_C * softplus_neg_lam
    row_g = jax.lax.broadcasted_iota(jnp.int32, (G, LANE), 0)

    def rows(k):
        return pl.ds(k, G, stride=SUBLANE)

    def rnn_head(hd):
        cols = head_cols(hd)
        ri = _sigmoid(_dot(vb_buf[:, cols], w_ax_ref[hd]) + b_ax_ref[hd:hd + 1, :])
        r = ri[:, 0:LANE]
        i = ri[:, LANE:2 * LANE]
        log_a = r * log_a_scale[:, cols]
        a = jnp.exp(log_a)
        b = _sqrt_nonneg(-jnp.tanh(log_a) * (1.0 + a * a)) * i * v_buf[:, cols]
        a_buf[hd] = a
        b_buf[hd] = b

        a_cum = a_buf[hd, rows(0), :]
        b_cum = b_buf[hd, rows(0), :]
        for k in range(1, SUBLANE):
            a_k = a_buf[hd, rows(k), :]
            b_cum = a_k * b_cum + b_buf[hd, rows(k), :]
            a_cum = a_k * a_cum
            a_buf[hd, rows(k), :] = a_cum
            b_buf[hd, rows(k), :] = b_cum
        d = 1
        while d < G:
            a_sh = jnp.where(row_g < d, 1.0, pltpu.roll(a_cum, d, axis=0))
            b_sh = jnp.where(row_g < d, 0.0, pltpu.roll(b_cum, d, axis=0))
            b_cum = a_cum * b_sh + b_cum
            a_cum = a_cum * a_sh
            d *= 2
        h0 = h_carry[hd:hd + 1, :]
        h_out = a_cum * h0 + b_cum
        h_carry[hd:hd + 1, :] = h_out[G - 1:G, :]
        h_in = jnp.where(row_g == 0, h0, pltpu.roll(h_out, 1, axis=0))
        for k in range(SUBLANE):
            b_buf[hd, rows(k), :] = b_buf[hd, rows(k), :] + a_buf[hd, rows(k), :] * h_in
        hg_buf[:, cols] = (b_buf[hd] * _gelu_tanh(gate_buf[:, cols])).astype(_BF16)

    assert len(FF_CHUNK_STARTS) == 6 and N_HEADS == 8
    ffn_chunk(0)
    h_buf[...] = _rmsnorm(x_ref[0], g_mix_ref[...]).astype(_BF16)
    rnn_ext[CONV_HALO:CONV_HALO + T, :] = _dot(h_buf[...], w_in_ref[:, O_RNN:O_GATE])
    pool_ext[POOL_HALO:POOL_HALO + T, :] = _dot(h_buf[...], w_in_ref[:, 0:O_RNN])
    gate_buf[...] = _dot(h_buf[...], w_in_ref[:, O_GATE:O_GPOOL])
    conv_all_heads()
    pool_windows()
    rnn_head(0)
    ffn_chunk(1)
    rnn_head(1)
    ffn_chunk(2)
    rnn_head(2)
    ffn_chunk(3)
    rnn_head(3)
    ffn_chunk(4)
    rnn_head(4)
    ffn_chunk(5)
    rnn_head(5)
    sg_pool_buf[...] = _sigmoid(_dot(h_buf[...], w_in_ref[:, O_GPOOL:O_GRNN]))
    rnn_head(6)
    pool_group_dots()
    sg_rnn_buf[...] = _sigmoid(_dot(h_buf[...], w_in_ref[:, O_GRNN:O_GRNN + D_MODEL]))
    rnn_head(7)
    y_pool = _dot(mixed_buf[...], w_pool_out_ref[...])
    ffn_out_first_half()
    y_rnn = _dot(hg_buf[...], w_rnn_out_ref[...])
    ffn_out_second_half()
    mix = sg_pool_buf[...] * y_pool + sg_rnn_buf[...] * y_rnn
    x1 = x_ref[0] + _dot(mix.astype(_BF16), w_o_ref[...])
    x1_buf[...] = x1
    h2_buf[...] = _rmsnorm(x1, g_ffn_ref[...]).astype(_BF16)


def _resident(shape):
    n = len(shape)
    return pl.BlockSpec(shape, lambda j: (0,) * n, pipeline_mode=pl.Buffered(1))


@jax.jit
def kernel(x, norm_mix, w_in, w_pool_grp, pool_scale, w_pool_out, conv_w, conv_b, w_rg_a, b_rg_a,
           w_rg_x, b_rg_x, lru_lambda, w_rnn_out, w_o, norm_ffn, w_ffn_in, w_ffn_out, norm_final):
    B, S, D = x.shape
    assert D == D_MODEL and S % SEQ_TILE == 0
    assert norm_mix.shape[0] == 1, "single-layer stack"
    T = SEQ_TILE
    nt = S // T
    n_tiles = B * nt

    w_ax = jnp.concatenate([w_rg_a[0], w_rg_x[0]], axis=-1).astype(_BF16)
    b_ax = jnp.concatenate([b_rg_a[0], b_rg_x[0]], axis=-1)
    operands = [
        x,
        norm_mix[0].reshape(1, D),
        w_in[0].astype(_BF16),
        w_pool_grp[0].astype(_BF16),
        pool_scale[0].reshape(1, D_POOL),
        w_pool_out[0].astype(_BF16),
        conv_w[0],
        conv_b[0].reshape(1, D_RNN),
        w_ax,
        b_ax,
        lru_lambda[0].reshape(1, D_RNN),
        w_rnn_out[0].astype(_BF16),
        w_o[0].astype(_BF16),
        norm_ffn[0].reshape(1, D),
        w_ffn_in[0].astype(_BF16),
        w_ffn_out[0].astype(_BF16),
        norm_final.reshape(1, D),
    ]

    def mixer_tile_index(j):
        m = jnp.minimum(j, n_tiles - 1)
        return (m // nt, m % nt, 0)

    def ffn_tile_index(j):
        m = jnp.maximum(j - 1, 0)
        return (m // nt, m % nt, 0)

    in_specs = [pl.BlockSpec((1, T, D), mixer_tile_index)]
    in_specs += [_resident(op.shape) for op in operands[1:]]

    return pl.pallas_call(
        functools.partial(_layer_kernel, nt),
        grid=(n_tiles + 1,),
        in_specs=in_specs,
        out_specs=pl.BlockSpec((1, T, D), ffn_tile_index),
        out_shape=jax.ShapeDtypeStruct((B, S, D), x.dtype),
        scratch_shapes=[
            pltpu.VMEM((POOL_HALO + T, D_POOL), _F32),
            pltpu.VMEM((CONV_HALO + T, D_RNN), _F32),
            pltpu.VMEM((T, D_RNN), _F32),
            pltpu.VMEM((T, D_RNN), _F32),
            pltpu.VMEM((T, D_RNN), _BF16),
            pltpu.VMEM((N_HEADS, T, LANE), _F32),
            pltpu.VMEM((N_HEADS, T, LANE), _F32),
            pltpu.VMEM((T, D_RNN), _BF16),
            pltpu.VMEM((T, D_FF), _BF16),
            pltpu.VMEM((T, D), _F32),
            pltpu.VMEM((T, D), _BF16),
            pltpu.VMEM((T, D), _BF16),
            pltpu.VMEM((T, D_POOL), _BF16),
            pltpu.VMEM((T, D_POOL), _BF16),
            pltpu.VMEM((T, D), _F32),
            pltpu.VMEM((T, D), _F32),
            pltpu.VMEM((T, D), _F32),
            pltpu.VMEM((N_HEADS, LANE), _F32),
        ],
        compiler_params=pltpu.CompilerParams(
            dimension_semantics=("arbitrary",),
            vmem_limit_bytes=VMEM_LIMIT_BYTES,
        ),
        name="hybrid_layer",
    )(*operands)
```

```python
import functools
import math

import jax
import jax.numpy as jnp
from jax.experimental import pallas as pl
from jax.experimental.pallas import tpu as pltpu

D_MODEL = 1024
POOL_WINDOWS = (2, 4, 8, 16)
LANE = 128
PHASES = 8
D_POOL = LANE * len(POOL_WINDOWS)
D_RNN = D_MODEL
N_HEADS = D_RNN // LANE
CONV_WIDTH = 4
LRU_C = 8.0
D_FF = 2816
NORM_EPS = 1e-6

SEQ_TILE = 256
GROUPS = SEQ_TILE // PHASES
FF_CHUNK = 512
FF_CHUNK_STARTS = tuple(range(0, D_FF, FF_CHUNK))
FF_OUT_SPLIT = 1536
POOL_SHIFTS = (1, 2, 4, 8)
VMEM_LIMIT_BYTES = 58 * 1024 * 1024

O_RNN = D_POOL
O_GATE = O_RNN + D_RNN
O_GPOOL = O_GATE + D_RNN
O_GRNN = O_GPOOL + D_MODEL

_F32 = jnp.float32
_BF16 = jnp.bfloat16


def _dot(a, b):
    return jnp.dot(a, b, preferred_element_type=_F32)


def _rmsnorm(x, g):
    return x * jax.lax.rsqrt(jnp.mean(x * x, axis=-1, keepdims=True) + NORM_EPS) * g


def _sigmoid(x):
    return 0.5 * jnp.tanh(0.5 * x) + 0.5


def _gelu_tanh(x):
    c = math.sqrt(2.0 / math.pi)
    return 0.5 * x * (1.0 + jnp.tanh(c * (x + 0.044715 * (x * x * x))))


def _sqrt_nonneg(x):
    return jnp.where(x > 0.0, x * jax.lax.rsqrt(x), 0.0)


def _block(k):
    return slice(k * GROUPS, (k + 1) * GROUPS)


def _layer_kernel(tiles_per_seq,
                  x_ref, g_mix_ref, w_in_ref, w_grp_ref, pool_scale_ref, w_pool_out_ref,
                  conv_w_ref, conv_b_ref, w_ax_ref, b_ax_ref, lam_ref, w_rnn_out_ref, w_o_ref,
                  g_ffn_ref, w_ffn_in_ref, w_ffn_out_ref, g_final_ref,
                  o_ref,
                  upool_buf, urnn_buf, gate_buf, v_buf, vb_buf, a_buf, b_buf, hg_buf, act_buf,
                  x1_buf, h_buf, h2_buf, pooled_buf, mixed_buf, sg_pool_buf, sg_rnn_buf,
                  ffn_acc, pool_carry, conv_carry, h_carry):
    T = SEQ_TILE
    G = GROUPS
    D = D_MODEL
    j = pl.program_id(0)
    s = j % tiles_per_seq

    @pl.when(j == 0)
    def _():
        x1_buf[...] = jnp.zeros_like(x1_buf)
        h2_buf[...] = jnp.zeros_like(h2_buf)

    @pl.when(s == 0)
    def _():
        pool_carry[...] = jnp.zeros_like(pool_carry)
        conv_carry[...] = jnp.zeros_like(conv_carry)
        h_carry[...] = jnp.zeros_like(h_carry)

    def head_cols(hd):
        return slice(hd * LANE, (hd + 1) * LANE)

    row_g = jax.lax.broadcasted_iota(jnp.int32, (G, LANE), 0)

    def previous_group(blk, last_row_before_tile):
        return jnp.where(row_g == 0, last_row_before_tile, pltpu.roll(blk, 1, axis=0))

    def x_tile():
        return jnp.concatenate([x_ref[0, 0, :, k * D:(k + 1) * D] for k in range(PHASES)], axis=0)

    def ffn_chunk(n):
        c = FF_CHUNK_STARTS[n]
        wd = min(FF_CHUNK, D_FF - c)
        gate = _dot(h2_buf[...], w_ffn_in_ref[:, c:c + wd])
        up = _dot(h2_buf[...], w_ffn_in_ref[:, D_FF + c:D_FF + c + wd])
        act_buf[:, c:c + wd] = (gate * _sigmoid(gate) * up).astype(_BF16)

    def ffn_out_first_half():
        ffn_acc[...] = x1_buf[...] + _dot(act_buf[:, 0:FF_OUT_SPLIT],
                                          w_ffn_out_ref[0:FF_OUT_SPLIT, :])

    def ffn_out_second_half():
        x2 = ffn_acc[...] + _dot(act_buf[:, FF_OUT_SPLIT:D_FF], w_ffn_out_ref[FF_OUT_SPLIT:D_FF, :])
        y = _rmsnorm(x2, g_final_ref[...])
        for k in range(PHASES):
            o_ref[0, 0, :, k * D:(k + 1) * D] = y[_block(k), :]

    def pool_windows():
        first_groups = jax.lax.broadcasted_iota(jnp.int32, (PHASES, LANE), 0)
        for g, w in enumerate(POOL_WINDOWS):
            cols = head_cols(g)
            tok = [upool_buf[_block(k), cols] for k in range(PHASES)]
            level = tok
            for i, d in enumerate(POOL_SHIFTS):
                if d >= w:
                    break
                shifted = []
                for k in range(PHASES):
                    if k >= d:
                        shifted.append(level[k - d])
                    else:
                        src = k - d + PHASES
                        row = i * PHASES + src
                        shifted.append(previous_group(level[src], pool_carry[row:row + 1, cols]))
                for src in range(PHASES - d, PHASES):
                    row = i * PHASES + src
                    pool_carry[row:row + 1, cols] = level[src][G - 1:G, :]
                level = [level[k] + shifted[k] for k in range(PHASES)]
            for k in range(PHASES):
                pos1 = s * T + PHASES * first_groups + (k + 1)
                inv_head = 1.0 / jnp.minimum(pos1, w).astype(_F32)
                inv_count = jnp.concatenate(
                    [inv_head, jnp.full((G - PHASES, LANE), 1.0 / w, _F32)], axis=0)
                pooled = level[k] * inv_count - tok[k]
                pooled_buf[_block(k), cols] = pooled.astype(_BF16)

    def pool_group_dots():
        for g in range(len(POOL_WINDOWS)):
            cols = head_cols(g)
            mixed = _dot(pooled_buf[:, cols], w_grp_ref[g]) * pool_scale_ref[:, cols]
            mixed_buf[:, cols] = mixed.astype(_BF16)

    def conv_all_heads():
        for hd in range(N_HEADS):
            cols = head_cols(hd)
            tok = [urnn_buf[_block(k), cols] for k in range(PHASES)]
            prev = {}
            for src in range(PHASES - CONV_WIDTH + 1, PHASES):
                prev[src] = previous_group(tok[src], conv_carry[src:src + 1, cols])
                conv_carry[src:src + 1, cols] = tok[src][G - 1:G, :]
            for k in range(PHASES):
                v = conv_b_ref[:, cols]
                for tap in range(CONV_WIDTH):
                    m = CONV_WIDTH - 1 - tap
                    u = tok[k - m] if k >= m else prev[k - m + PHASES]
                    v = v + u * conv_w_ref[tap:tap + 1, cols]
                v_buf[_block(k), cols] = v
                vb_buf[_block(k), cols] = v.astype(_BF16)

    lam = lam_ref[...]
    softplus_neg_lam = jnp.maximum(-lam, 0.0) + jnp.log1p(jnp.exp(-jnp.abs(lam)))
    log_a_scale = -LRU_C * softplus_neg_lam

    def rnn_head(hd):
        cols = head_cols(hd)
        ri = _sigmoid(_dot(vb_buf[:, cols], w_ax_ref[hd]) + b_ax_ref[hd:hd + 1, :])
        r = ri[:, 0:LANE]
        i = ri[:, LANE:2 * LANE]
        log_a = r * log_a_scale[:, cols]
        a = jnp.exp(log_a)
        b = _sqrt_nonneg(-jnp.tanh(log_a) * (1.0 + a * a)) * i * v_buf[:, cols]
        a_buf[hd] = a
        b_buf[hd] = b

        a_cum = a_buf[hd, _block(0), :]
        b_cum = b_buf[hd, _block(0), :]
        for k in range(1, PHASES):
            a_k = a_buf[hd, _block(k), :]
            b_cum = a_k * b_cum + b_buf[hd, _block(k), :]
            a_cum = a_k * a_cum
            a_buf[hd, _block(k), :] = a_cum
            b_buf[hd, _block(k), :] = b_cum
        d = 1
        while d < G:
            a_sh = jnp.where(row_g < d, 1.0, pltpu.roll(a_cum, d, axis=0))
            b_sh = jnp.where(row_g < d, 0.0, pltpu.roll(b_cum, d, axis=0))
            b_cum = a_cum * b_sh + b_cum
            a_cum = a_cum * a_sh
            d *= 2
        h0 = h_carry[hd:hd + 1, :]
        h_out = a_cum * h0 + b_cum
        h_carry[hd:hd + 1, :] = h_out[G - 1:G, :]
        h_in = previous_group(h_out, h0)
        for k in range(PHASES):
            h_k = b_buf[hd, _block(k), :] + a_buf[hd, _block(k), :] * h_in
            hg_buf[_block(k), cols] = (h_k * _gelu_tanh(gate_buf[_block(k), cols])).astype(_BF16)

    assert len(FF_CHUNK_STARTS) == 6 and N_HEADS == 8
    ffn_chunk(0)
    h_buf[...] = _rmsnorm(x_tile(), g_mix_ref[...]).astype(_BF16)
    urnn_buf[...] = _dot(h_buf[...], w_in_ref[:, O_RNN:O_GATE])
    upool_buf[...] = _dot(h_buf[...], w_in_ref[:, 0:O_RNN])
    gate_buf[...] = _dot(h_buf[...], w_in_ref[:, O_GATE:O_GPOOL])
    conv_all_heads()
    pool_windows()
    rnn_head(0)
    ffn_chunk(1)
    rnn_head(1)
    ffn_chunk(2)
    rnn_head(2)
    ffn_chunk(3)
    rnn_head(3)
    ffn_chunk(4)
    rnn_head(4)
    ffn_chunk(5)
    rnn_head(5)
    sg_pool_buf[...] = _sigmoid(_dot(h_buf[...], w_in_ref[:, O_GPOOL:O_GRNN]))
    rnn_head(6)
    pool_group_dots()
    sg_rnn_buf[...] = _sigmoid(_dot(h_buf[...], w_in_ref[:, O_GRNN:O_GRNN + D_MODEL]))
    rnn_head(7)
    y_pool = _dot(mixed_buf[...], w_pool_out_ref[...])
    ffn_out_first_half()
    y_rnn = _dot(hg_buf[...], w_rnn_out_ref[...])
    ffn_out_second_half()
    mix = sg_pool_buf[...] * y_pool + sg_rnn_buf[...] * y_rnn
    x1 = x_tile() + _dot(mix.astype(_BF16), w_o_ref[...])
    x1_buf[...] = x1
    h2_buf[...] = _rmsnorm(x1, g_ffn_ref[...]).astype(_BF16)


def _resident(shape):
    n = len(shape)
    return pl.BlockSpec(shape, lambda j: (0,) * n, pipeline_mode=pl.Buffered(1))


@jax.jit
def kernel(x, norm_mix, w_in, w_pool_grp, pool_scale, w_pool_out, conv_w, conv_b, w_rg_a, b_rg_a,
           w_rg_x, b_rg_x, lru_lambda, w_rnn_out, w_o, norm_ffn, w_ffn_in, w_ffn_out, norm_final):
    B, S, D = x.shape
    assert D == D_MODEL and S % SEQ_TILE == 0
    assert norm_mix.shape[0] == 1, "single-layer stack"
    T = SEQ_TILE
    nt = S // T
    n_tiles = B * nt

    w_ax = jnp.concatenate([w_rg_a[0], w_rg_x[0]], axis=-1).astype(_BF16)
    b_ax = jnp.concatenate([b_rg_a[0], b_rg_x[0]], axis=-1)
    operands = [
        x.reshape(B, nt, GROUPS, PHASES * D),
        norm_mix[0].reshape(1, D),
        w_in[0].astype(_BF16),
        w_pool_grp[0].astype(_BF16),
        pool_scale[0].reshape(1, D_POOL),
        w_pool_out[0].astype(_BF16),
        conv_w[0],
        conv_b[0].reshape(1, D_RNN),
        w_ax,
        b_ax,
        lru_lambda[0].reshape(1, D_RNN),
        w_rnn_out[0].astype(_BF16),
        w_o[0].astype(_BF16),
        norm_ffn[0].reshape(1, D),
        w_ffn_in[0].astype(_BF16),
        w_ffn_out[0].astype(_BF16),
        norm_final.reshape(1, D),
    ]

    def mixer_tile_index(j):
        m = jnp.minimum(j, n_tiles - 1)
        return (m // nt, m % nt, 0, 0)

    def ffn_tile_index(j):
        m = jnp.maximum(j - 1, 0)
        return (m // nt, m % nt, 0, 0)

    tile_block = (1, 1, GROUPS, PHASES * D)
    in_specs = [pl.BlockSpec(tile_block, mixer_tile_index)]
    in_specs += [_resident(op.shape) for op in operands[1:]]
    n_carry_rows = len(POOL_SHIFTS) * PHASES

    out = pl.pallas_call(
        functools.partial(_layer_kernel, nt),
        grid=(n_tiles + 1,),
        in_specs=in_specs,
        out_specs=pl.BlockSpec(tile_block, ffn_tile_index),
        out_shape=jax.ShapeDtypeStruct((B, nt, GROUPS, PHASES * D), x.dtype),
        scratch_shapes=[
            pltpu.VMEM((T, D_POOL), _F32),
            pltpu.VMEM((T, D_RNN), _F32),
            pltpu.VMEM((T, D_RNN), _F32),
            pltpu.VMEM((T, D_RNN), _F32),
            pltpu.VMEM((T, D_RNN), _BF16),
            pltpu.VMEM((N_HEADS, T, LANE), _F32),
            pltpu.VMEM((N_HEADS, T, LANE), _F32),
            pltpu.VMEM((T, D_RNN), _BF16),
            pltpu.VMEM((T, D_FF), _BF16),
            pltpu.VMEM((T, D), _F32),
            pltpu.VMEM((T, D), _BF16),
            pltpu.VMEM((T, D), _BF16),
            pltpu.VMEM((T, D_POOL), _BF16),
            pltpu.VMEM((T, D_POOL), _BF16),
            pltpu.VMEM((T, D), _F32),
            pltpu.VMEM((T, D), _F32),
            pltpu.VMEM((T, D), _F32),
            pltpu.VMEM((n_carry_rows, D_POOL), _F32),
            pltpu.VMEM((PHASES, D_RNN), _F32),
            pltpu.VMEM((N_HEADS, LANE), _F32),
        ],
        compiler_params=pltpu.CompilerParams(
            dimension_semantics=("arbitrary",),
            vmem_limit_bytes=VMEM_LIMIT_BYTES,
        ),
        name="hybrid_layer",
    )(*operands)
    return out.reshape(B, S, D)
```

```python
import functools
import math

import jax
import jax.numpy as jnp
from jax.experimental import pallas as pl
from jax.experimental.pallas import tpu as pltpu

D_MODEL = 1024
POOL_WINDOWS = (2, 4, 8, 16)
LANE = 128
PHASES = 8
D_POOL = LANE * len(POOL_WINDOWS)
D_RNN = D_MODEL
N_HEADS = D_RNN // LANE
CONV_WIDTH = 4
LRU_C = 8.0
D_FF = 2816
NORM_EPS = 1e-6

SEQ_TILE = 256
GROUPS = SEQ_TILE // PHASES
N_SLOTS = 2
FF_CHUNK = 512
FF_CHUNK_STARTS = tuple(range(0, D_FF, FF_CHUNK))
FF_OUT_SPLIT = 1536
POOL_SHIFTS = (1, 2, 4, 8)
VMEM_LIMIT_BYTES = 58 * 1024 * 1024

O_RNN = D_POOL
O_GATE = O_RNN + D_RNN
O_GPOOL = O_GATE + D_RNN
O_GRNN = O_GPOOL + D_MODEL

_F32 = jnp.float32
_BF16 = jnp.bfloat16


def _dot(a, b):
    return jnp.dot(a, b, preferred_element_type=_F32)


def _rmsnorm(x, g):
    return x * jax.lax.rsqrt(jnp.mean(x * x, axis=-1, keepdims=True) + NORM_EPS) * g


def _sigmoid(x):
    return 0.5 * jnp.tanh(0.5 * x) + 0.5


def _gelu_tanh(x):
    c = math.sqrt(2.0 / math.pi)
    return 0.5 * x * (1.0 + jnp.tanh(c * (x + 0.044715 * (x * x * x))))


def _sqrt_nonneg(x):
    return jnp.where(x > 0.0, x * jax.lax.rsqrt(x), 0.0)


def _block(k):
    return slice(k * GROUPS, (k + 1) * GROUPS)


def _layer_kernel(tiles_per_seq, n_tiles,
                  x_hbm, g_mix_ref, w_in_ref, w_grp_ref, pool_scale_ref, w_pool_out_ref,
                  conv_w_ref, conv_b_ref, w_ax_ref, b_ax_ref, lam_ref, w_rnn_out_ref, w_o_ref,
                  g_ffn_ref, w_ffn_in_ref, w_ffn_out_ref, g_final_ref,
                  o_hbm,
                  x_buf, y_buf, in_sem, out_sem,
                  upool_buf, urnn_buf, gate_buf, v_buf, vb_buf, a_buf, b_buf, hg_buf, act_buf,
                  x1_buf, h_buf, h2_buf, pooled_buf, mixed_buf, sg_pool_buf, sg_rnn_buf,
                  ffn_acc, pool_carry, conv_carry, h_carry):
    T = SEQ_TILE
    G = GROUPS
    j = pl.program_id(0)
    s = j % tiles_per_seq
    x_slot = j % N_SLOTS
    y_slot = (j + 1) % N_SLOTS

    def tile_phase_hbm(ref, t, k):
        return ref.at[t // tiles_per_seq, pl.ds((t % tiles_per_seq) * G, G), k, :]

    def x_copy(t, k):
        return pltpu.make_async_copy(tile_phase_hbm(x_hbm, t, k),
                                     x_buf.at[t % N_SLOTS, pl.ds(k * G, G), :],
                                     in_sem.at[t % N_SLOTS])

    def y_copy(t, k):
        return pltpu.make_async_copy(y_buf.at[t % N_SLOTS, pl.ds(k * G, G), :],
                                     tile_phase_hbm(o_hbm, t, k),
                                     out_sem.at[t % N_SLOTS])

    def for_phases(copy, t, action):
        for k in range(PHASES):
            getattr(copy(t, k), action)()

    @pl.when(j == 0)
    def _():
        for_phases(x_copy, 0, "start")
        x1_buf[...] = jnp.zeros_like(x1_buf)
        h2_buf[...] = jnp.zeros_like(h2_buf)

    @pl.when(j + 1 < n_tiles)
    def _():
        for_phases(x_copy, j + 1, "start")

    @pl.when(j < n_tiles)
    def _():
        for_phases(x_copy, j, "wait")

    @pl.when(j >= N_SLOTS + 1)
    def _():
        for_phases(y_copy, j - 1 - N_SLOTS, "wait")

    @pl.when(s == 0)
    def _():
        pool_carry[...] = jnp.zeros_like(pool_carry)
        conv_carry[...] = jnp.zeros_like(conv_carry)
        h_carry[...] = jnp.zeros_like(h_carry)

    def head_cols(hd):
        return slice(hd * LANE, (hd + 1) * LANE)

    row_g = jax.lax.broadcasted_iota(jnp.int32, (G, LANE), 0)

    def previous_group(blk, last_row_before_tile):
        return jnp.where(row_g == 0, last_row_before_tile, pltpu.roll(blk, 1, axis=0))

    def ffn_chunk(n):
        c = FF_CHUNK_STARTS[n]
        wd = min(FF_CHUNK, D_FF - c)
        gate = _dot(h2_buf[...], w_ffn_in_ref[:, c:c + wd])
        up = _dot(h2_buf[...], w_ffn_in_ref[:, D_FF + c:D_FF + c + wd])
        act_buf[:, c:c + wd] = (gate * _sigmoid(gate) * up).astype(_BF16)

    def ffn_out_first_half():
        ffn_acc[...] = x1_buf[...] + _dot(act_buf[:, 0:FF_OUT_SPLIT],
                                          w_ffn_out_ref[0:FF_OUT_SPLIT, :])

    def ffn_out_second_half():
        x2 = ffn_acc[...] + _dot(act_buf[:, FF_OUT_SPLIT:D_FF], w_ffn_out_ref[FF_OUT_SPLIT:D_FF, :])
        y_buf[y_slot] = _rmsnorm(x2, g_final_ref[...])

    def pool_windows():
        first_groups = jax.lax.broadcasted_iota(jnp.int32, (PHASES, LANE), 0)
        for g, w in enumerate(POOL_WINDOWS):
            cols = head_cols(g)
            tok = [upool_buf[_block(k), cols] for k in range(PHASES)]
            level = tok
            for i, d in enumerate(POOL_SHIFTS):
                if d >= w:
                    break
                shifted = []
                for k in range(PHASES):
                    if k >= d:
                        shifted.append(level[k - d])
                    else:
                        src = k - d + PHASES
                        row = i * PHASES + src
                        shifted.append(previous_group(level[src], pool_carry[row:row + 1, cols]))
                for src in range(PHASES - d, PHASES):
                    row = i * PHASES + src
                    pool_carry[row:row + 1, cols] = level[src][G - 1:G, :]
                level = [level[k] + shifted[k] for k in range(PHASES)]
            for k in range(PHASES):
                pos1 = s * T + PHASES * first_groups + (k + 1)
                inv_head = 1.0 / jnp.minimum(pos1, w).astype(_F32)
                inv_count = jnp.concatenate(
                    [inv_head, jnp.full((G - PHASES, LANE), 1.0 / w, _F32)], axis=0)
                pooled = level[k] * inv_count - tok[k]
                pooled_buf[_block(k), cols] = pooled.astype(_BF16)

    def pool_group_dots():
        for g in range(len(POOL_WINDOWS)):
            cols = head_cols(g)
            mixed = _dot(pooled_buf[:, cols], w_grp_ref[g]) * pool_scale_ref[:, cols]
            mixed_buf[:, cols] = mixed.astype(_BF16)

    def conv_all_heads():
        for hd in range(N_HEADS):
            cols = head_cols(hd)
            tok = [urnn_buf[_block(k), cols] for k in range(PHASES)]
            prev = {}
            for src in range(PHASES - CONV_WIDTH + 1, PHASES):
                prev[src] = previous_group(tok[src], conv_carry[src:src + 1, cols])
                conv_carry[src:src + 1, cols] = tok[src][G - 1:G, :]
            for k in range(PHASES):
                v = conv_b_ref[:, cols]
                for tap in range(CONV_WIDTH):
                    m = CONV_WIDTH - 1 - tap
                    u = tok[k - m] if k >= m else prev[k - m + PHASES]
                    v = v + u * conv_w_ref[tap:tap + 1, cols]
                v_buf[_block(k), cols] = v
                vb_buf[_block(k), cols] = v.astype(_BF16)

    lam = lam_ref[...]
    softplus_neg_lam = jnp.maximum(-lam, 0.0) + jnp.log1p(jnp.exp(-jnp.abs(lam)))
    log_a_scale = -LRU_C * softplus_neg_lam

    def rnn_head(hd):
        cols = head_cols(hd)
        ri = _sigmoid(_dot(vb_buf[:, cols], w_ax_ref[hd]) + b_ax_ref[hd:hd + 1, :])
        r = ri[:, 0:LANE]
        i = ri[:, LANE:2 * LANE]
        log_a = r * log_a_scale[:, cols]
        a = jnp.exp(log_a)
        b = _sqrt_nonneg(-jnp.tanh(log_a) * (1.0 + a * a)) * i * v_buf[:, cols]
        a_buf[hd] = a
        b_buf[hd] = b

        a_cum = a_buf[hd, _block(0), :]
        b_cum = b_buf[hd, _block(0), :]
        for k in range(1, PHASES):
            a_k = a_buf[hd, _block(k), :]
            b_cum = a_k * b_cum + b_buf[hd, _block(k), :]
            a_cum = a_k * a_cum
            a_buf[hd, _block(k), :] = a_cum
            b_buf[hd, _block(k), :] = b_cum
        d = 1
        while d < G:
            a_sh = jnp.where(row_g < d, 1.0, pltpu.roll(a_cum, d, axis=0))
            b_sh = jnp.where(row_g < d, 0.0, pltpu.roll(b_cum, d, axis=0))
            b_cum = a_cum * b_sh + b_cum
            a_cum = a_cum * a_sh
            d *= 2
        h0 = h_carry[hd:hd + 1, :]
        h_out = a_cum * h0 + b_cum
        h_carry[hd:hd + 1, :] = h_out[G - 1:G, :]
        h_in = previous_group(h_out, h0)
        for k in range(PHASES):
            h_k = b_buf[hd, _block(k), :] + a_buf[hd, _block(k), :] * h_in
            hg_buf[_block(k), cols] = (h_k * _gelu_tanh(gate_buf[_block(k), cols])).astype(_BF16)

    assert len(FF_CHUNK_STARTS) == 6 and N_HEADS == 8
    ffn_chunk(0)
    h_buf[...] = _rmsnorm(x_buf[x_slot], g_mix_ref[...]).astype(_BF16)
    urnn_buf[...] = _dot(h_buf[...], w_in_ref[:, O_RNN:O_GATE])
    upool_buf[...] = _dot(h_buf[...], w_in_ref[:, 0:O_RNN])
    gate_buf[...] = _dot(h_buf[...], w_in_ref[:, O_GATE:O_GPOOL])
    conv_all_heads()
    pool_windows()
    rnn_head(0)
    ffn_chunk(1)
    rnn_head(1)
    ffn_chunk(2)
    rnn_head(2)
    ffn_chunk(3)
    rnn_head(3)
    ffn_chunk(4)
    rnn_head(4)
    ffn_chunk(5)
    rnn_head(5)
    sg_pool_buf[...] = _sigmoid(_dot(h_buf[...], w_in_ref[:, O_GPOOL:O_GRNN]))
    rnn_head(6)
    pool_group_dots()
    sg_rnn_buf[...] = _sigmoid(_dot(h_buf[...], w_in_ref[:, O_GRNN:O_GRNN + D_MODEL]))
    rnn_head(7)
    y_pool = _dot(mixed_buf[...], w_pool_out_ref[...])
    ffn_out_first_half()
    y_rnn = _dot(hg_buf[...], w_rnn_out_ref[...])
    ffn_out_second_half()
    mix = sg_pool_buf[...] * y_pool + sg_rnn_buf[...] * y_rnn
    x1 = x_buf[x_slot] + _dot(mix.astype(_BF16), w_o_ref[...])
    x1_buf[...] = x1
    h2_buf[...] = _rmsnorm(x1, g_ffn_ref[...]).astype(_BF16)

    @pl.when(j >= 1)
    def _():
        for_phases(y_copy, j - 1, "start")

    @pl.when(j == n_tiles)
    def _():
        for_phases(y_copy, n_tiles - 2, "wait")
        for_phases(y_copy, n_tiles - 1, "wait")


def _resident(shape):
    n = len(shape)
    return pl.BlockSpec(shape, lambda j: (0,) * n, pipeline_mode=pl.Buffered(1))


@jax.jit
def kernel(x, norm_mix, w_in, w_pool_grp, pool_scale, w_pool_out, conv_w, conv_b, w_rg_a, b_rg_a,
           w_rg_x, b_rg_x, lru_lambda, w_rnn_out, w_o, norm_ffn, w_ffn_in, w_ffn_out, norm_final):
    B, S, D = x.shape
    assert D == D_MODEL and S % SEQ_TILE == 0
    assert norm_mix.shape[0] == 1, "single-layer stack"
    T = SEQ_TILE
    nt = S // T
    n_tiles = B * nt
    assert n_tiles >= N_SLOTS

    w_ax = jnp.concatenate([w_rg_a[0], w_rg_x[0]], axis=-1).astype(_BF16)
    b_ax = jnp.concatenate([b_rg_a[0], b_rg_x[0]], axis=-1)
    operands = [
        x.reshape(B, S // PHASES, PHASES, D),
        norm_mix[0].reshape(1, D),
        w_in[0].astype(_BF16),
        w_pool_grp[0].astype(_BF16),
        pool_scale[0].reshape(1, D_POOL),
        w_pool_out[0].astype(_BF16),
        conv_w[0],
        conv_b[0].reshape(1, D_RNN),
        w_ax,
        b_ax,
        lru_lambda[0].reshape(1, D_RNN),
        w_rnn_out[0].astype(_BF16),
        w_o[0].astype(_BF16),
        norm_ffn[0].reshape(1, D),
        w_ffn_in[0].astype(_BF16),
        w_ffn_out[0].astype(_BF16),
        norm_final.reshape(1, D),
    ]
    in_specs = [pl.BlockSpec(memory_space=pl.ANY)]
    in_specs += [_resident(op.shape) for op in operands[1:]]
    n_carry_rows = len(POOL_SHIFTS) * PHASES

    out = pl.pallas_call(
        functools.partial(_layer_kernel, nt, n_tiles),
        grid=(n_tiles + 1,),
        in_specs=in_specs,
        out_specs=pl.BlockSpec(memory_space=pl.ANY),
        out_shape=jax.ShapeDtypeStruct((B, S // PHASES, PHASES, D), x.dtype),
        scratch_shapes=[
            pltpu.VMEM((N_SLOTS, T, D), _F32),
            pltpu.VMEM((N_SLOTS, T, D), _F32),
            pltpu.SemaphoreType.DMA((N_SLOTS,)),
            pltpu.SemaphoreType.DMA((N_SLOTS,)),
            pltpu.VMEM((T, D_POOL), _F32),
            pltpu.VMEM((T, D_RNN), _F32),
            pltpu.VMEM((T, D_RNN), _F32),
            pltpu.VMEM((T, D_RNN), _F32),
            pltpu.VMEM((T, D_RNN), _BF16),
            pltpu.VMEM((N_HEADS, T, LANE), _F32),
            pltpu.VMEM((N_HEADS, T, LANE), _F32),
            pltpu.VMEM((T, D_RNN), _BF16),
            pltpu.VMEM((T, D_FF), _BF16),
            pltpu.VMEM((T, D), _F32),
            pltpu.VMEM((T, D), _BF16),
            pltpu.VMEM((T, D), _BF16),
            pltpu.VMEM((T, D_POOL), _BF16),
            pltpu.VMEM((T, D_POOL), _BF16),
            pltpu.VMEM((T, D), _F32),
            pltpu.VMEM((T, D), _F32),
            pltpu.VMEM((T, D), _F32),
            pltpu.VMEM((n_carry_rows, D_POOL), _F32),
            pltpu.VMEM((PHASES, D_RNN), _F32),
            pltpu.VMEM((N_HEADS, LANE), _F32),
        ],
        compiler_params=pltpu.CompilerParams(
            dimension_semantics=("arbitrary",),
            vmem_limit_bytes=VMEM_LIMIT_BYTES,
        ),
        name="hybrid_layer",
    )(*operands)
    return out.reshape(B, S, D)
```

```python
import functools
import math

import jax
import jax.numpy as jnp
from jax.experimental import pallas as pl
from jax.experimental.pallas import tpu as pltpu

D_MODEL = 1024
POOL_WINDOWS = (2, 4, 8, 16)
LANE = 128
PHASES = 8
D_POOL = LANE * len(POOL_WINDOWS)
D_RNN = D_MODEL
N_HEADS = D_RNN // LANE
CONV_WIDTH = 4
LRU_C = 8.0
D_FF = 2816
NORM_EPS = 1e-6

SEQ_TILE = 256
GROUPS = SEQ_TILE // PHASES
N_SLOTS = 2
FF_CHUNK = 512
FF_CHUNK_STARTS = tuple(range(0, D_FF, FF_CHUNK))
FF_OUT_SPLIT = 1536
POOL_SHIFTS = (1, 2, 4, 8)
WIDE_STAGE_ROWS = 32
WIDE_STAGE_SLOTS = 6
NARROW_STAGE_ROWS = 256
NARROW_STAGE_SLOTS = 4
VMEM_LIMIT_BYTES = 58 * 1024 * 1024

O_RNN = D_POOL
O_GATE = O_RNN + D_RNN
O_GPOOL = O_GATE + D_RNN
O_GRNN = O_GPOOL + D_MODEL

_F32 = jnp.float32
_BF16 = jnp.bfloat16


def _dot(a, b):
    return jnp.dot(a, b, preferred_element_type=_F32)


def _rmsnorm(x, g):
    return x * jax.lax.rsqrt(jnp.mean(x * x, axis=-1, keepdims=True) + NORM_EPS) * g


def _sigmoid(x):
    return 0.5 * jnp.tanh(0.5 * x) + 0.5


def _gelu_tanh(x):
    c = math.sqrt(2.0 / math.pi)
    return 0.5 * x * (1.0 + jnp.tanh(c * (x + 0.044715 * (x * x * x))))


def _sqrt_nonneg(x):
    return jnp.where(x > 0.0, x * jax.lax.rsqrt(x), 0.0)


def _block(k):
    return slice(k * GROUPS, (k + 1) * GROUPS)


def _load_weights_as_bf16(jobs, stage, sem):
    n_slots = stage.shape[0]

    def copy(i):
        src, r0, rows, cols, _, _ = jobs[i]
        return pltpu.make_async_copy(src.at[pl.ds(r0, rows), :],
                                     stage.at[i % n_slots, pl.ds(0, rows), pl.ds(0, cols)],
                                     sem.at[i % n_slots])

    for i in range(min(n_slots, len(jobs))):
        copy(i).start()
    for i, (_, r0, rows, cols, dst, c0) in enumerate(jobs):
        copy(i).wait()
        dst[r0:r0 + rows, c0:c0 + cols] = stage[i % n_slots, 0:rows, 0:cols].astype(_BF16)
        if i + n_slots < len(jobs):
            copy(i + n_slots).start()


def _row_chunks(src, dst, rows_per_chunk, dst_first_col=0):
    n_rows, n_cols = src.shape
    assert n_rows % rows_per_chunk == 0
    return [(src, r0, rows_per_chunk, n_cols, dst, dst_first_col)
            for r0 in range(0, n_rows, rows_per_chunk)]


def _layer_kernel(tiles_per_seq, n_tiles,
                  x_hbm, g_mix_ref, w_in_hbm, w_grp_hbm, pool_scale_ref, w_pool_out_hbm,
                  conv_w_ref, conv_b_ref, w_rg_a_hbm, w_rg_x_hbm, b_ax_ref, lam_ref,
                  w_rnn_out_hbm, w_o_hbm, g_ffn_ref, w_ffn_in_hbm, w_ffn_out_hbm, g_final_ref,
                  o_hbm,
                  x_buf, y_buf, in_sem, out_sem,
                  w_in_ref, w_grp_ref, w_pool_out_ref, w_ax_ref, w_rnn_out_ref, w_o_ref,
                  w_ffn_in_ref, w_ffn_out_ref, stage_wide, stage_narrow, wide_sem, narrow_sem,
                  upool_buf, urnn_buf, gate_buf, v_buf, vb_buf, a_buf, b_buf, hg_buf, act_buf,
                  x1_buf, h_buf, h2_buf, pooled_buf, mixed_buf, sg_pool_buf, sg_rnn_buf,
                  ffn_acc, pool_carry, conv_carry, h_carry):
    T = SEQ_TILE
    G = GROUPS
    j = pl.program_id(0)
    s = j % tiles_per_seq
    x_slot = j % N_SLOTS
    y_slot = (j + 1) % N_SLOTS

    def tile_phase_hbm(ref, t, k):
        return ref.at[t // tiles_per_seq, pl.ds((t % tiles_per_seq) * G, G), k, :]

    def x_copy(t, k):
        return pltpu.make_async_copy(tile_phase_hbm(x_hbm, t, k),
                                     x_buf.at[t % N_SLOTS, pl.ds(k * G, G), :],
                                     in_sem.at[t % N_SLOTS])

    def y_copy(t, k):
        return pltpu.make_async_copy(y_buf.at[t % N_SLOTS, pl.ds(k * G, G), :],
                                     tile_phase_hbm(o_hbm, t, k),
                                     out_sem.at[t % N_SLOTS])

    def for_phases(copy, t, action):
        for k in range(PHASES):
            getattr(copy(t, k), action)()

    @pl.when(j == 0)
    def _():
        for_phases(x_copy, 0, "start")
        x1_buf[...] = jnp.zeros_like(x1_buf)
        h2_buf[...] = jnp.zeros_like(h2_buf)
        _load_weights_as_bf16(
            _row_chunks(w_in_hbm, w_in_ref, WIDE_STAGE_ROWS)
            + _row_chunks(w_ffn_in_hbm, w_ffn_in_ref, WIDE_STAGE_ROWS),
            stage_wide, wide_sem)
        _load_weights_as_bf16(
            _row_chunks(w_ffn_out_hbm, w_ffn_out_ref, NARROW_STAGE_ROWS)
            + _row_chunks(w_rnn_out_hbm, w_rnn_out_ref, NARROW_STAGE_ROWS)
            + _row_chunks(w_o_hbm, w_o_ref, NARROW_STAGE_ROWS)
            + _row_chunks(w_pool_out_hbm, w_pool_out_ref, NARROW_STAGE_ROWS)
            + _row_chunks(w_rg_a_hbm, w_ax_ref, NARROW_STAGE_ROWS)
            + _row_chunks(w_rg_x_hbm, w_ax_ref, NARROW_STAGE_ROWS, dst_first_col=LANE)
            + _row_chunks(w_grp_hbm, w_grp_ref, NARROW_STAGE_ROWS),
            stage_narrow, narrow_sem)

    @pl.when(j + 1 < n_tiles)
    def _():
        for_phases(x_copy, j + 1, "start")

    @pl.when(j < n_tiles)
    def _():
        for_phases(x_copy, j, "wait")

    @pl.when(j >= N_SLOTS + 1)
    def _():
        for_phases(y_copy, j - 1 - N_SLOTS, "wait")

    @pl.when(s == 0)
    def _():
        pool_carry[...] = jnp.zeros_like(pool_carry)
        conv_carry[...] = jnp.zeros_like(conv_carry)
        h_carry[...] = jnp.zeros_like(h_carry)

    def head_cols(hd):
        return slice(hd * LANE, (hd + 1) * LANE)

    row_g = jax.lax.broadcasted_iota(jnp.int32, (G, LANE), 0)

    def previous_group(blk, last_row_before_tile):
        return jnp.where(row_g == 0, last_row_before_tile, pltpu.roll(blk, 1, axis=0))

    def ffn_chunk(n):
        c = FF_CHUNK_STARTS[n]
        wd = min(FF_CHUNK, D_FF - c)
        gate = _dot(h2_buf[...], w_ffn_in_ref[:, c:c + wd])
        up = _dot(h2_buf[...], w_ffn_in_ref[:, D_FF + c:D_FF + c + wd])
        act_buf[:, c:c + wd] = (gate * _sigmoid(gate) * up).astype(_BF16)

    def ffn_out_first_half():
        ffn_acc[...] = x1_buf[...] + _dot(act_buf[:, 0:FF_OUT_SPLIT],
                                          w_ffn_out_ref[0:FF_OUT_SPLIT, :])

    def ffn_out_second_half():
        x2 = ffn_acc[...] + _dot(act_buf[:, FF_OUT_SPLIT:D_FF], w_ffn_out_ref[FF_OUT_SPLIT:D_FF, :])
        y_buf[y_slot] = _rmsnorm(x2, g_final_ref[...])

    def pool_windows():
        first_groups = jax.lax.broadcasted_iota(jnp.int32, (PHASES, LANE), 0)
        for g, w in enumerate(POOL_WINDOWS):
            cols = head_cols(g)
            tok = [upool_buf[_block(k), cols] for k in range(PHASES)]
            level = tok
            for i, d in enumerate(POOL_SHIFTS):
                if d >= w:
                    break
                shifted = []
                for k in range(PHASES):
                    if k >= d:
                        shifted.append(level[k - d])
                    else:
                        src = k - d + PHASES
                        row = i * PHASES + src
                        shifted.append(previous_group(level[src], pool_carry[row:row + 1, cols]))
                for src in range(PHASES - d, PHASES):
                    row = i * PHASES + src
                    pool_carry[row:row + 1, cols] = level[src][G - 1:G, :]
                level = [level[k] + shifted[k] for k in range(PHASES)]
            for k in range(PHASES):
                pos1 = s * T + PHASES * first_groups + (k + 1)
                inv_head = 1.0 / jnp.minimum(pos1, w).astype(_F32)
                inv_count = jnp.concatenate(
                    [inv_head, jnp.full((G - PHASES, LANE), 1.0 / w, _F32)], axis=0)
                pooled = level[k] * inv_count - tok[k]
                pooled_buf[_block(k), cols] = pooled.astype(_BF16)

    def pool_group_dots():
        for g in range(len(POOL_WINDOWS)):
            cols = head_cols(g)
            mixed = _dot(pooled_buf[:, cols], w_grp_ref[cols, :]) * pool_scale_ref[:, cols]
            mixed_buf[:, cols] = mixed.astype(_BF16)

    def conv_all_heads():
        for hd in range(N_HEADS):
            cols = head_cols(hd)
            tok = [urnn_buf[_block(k), cols] for k in range(PHASES)]
            prev = {}
            for src in range(PHASES - CONV_WIDTH + 1, PHASES):
                prev[src] = previous_group(tok[src], conv_carry[src:src + 1, cols])
                conv_carry[src:src + 1, cols] = tok[src][G - 1:G, :]
            for k in range(PHASES):
                v = conv_b_ref[:, cols]
                for tap in range(CONV_WIDTH):
                    m = CONV_WIDTH - 1 - tap
                    u = tok[k - m] if k >= m else prev[k - m + PHASES]
                    v = v + u * conv_w_ref[tap:tap + 1, cols]
                v_buf[_block(k), cols] = v
                vb_buf[_block(k), cols] = v.astype(_BF16)

    lam = lam_ref[...]
    softplus_neg_lam = jnp.maximum(-lam, 0.0) + jnp.log1p(jnp.exp(-jnp.abs(lam)))
    log_a_scale = -LRU_C * softplus_neg_lam

    def rnn_head(hd):
        cols = head_cols(hd)
        ri = _sigmoid(_dot(vb_buf[:, cols], w_ax_ref[cols, :]) + b_ax_ref[hd:hd + 1, :])
        r = ri[:, 0:LANE]
        i = ri[:, LANE:2 * LANE]
        log_a = r * log_a_scale[:, cols]
        a = jnp.exp(log_a)
        b = _sqrt_nonneg(-jnp.tanh(log_a) * (1.0 + a * a)) * i * v_buf[:, cols]
        a_buf[hd] = a
        b_buf[hd] = b

        a_cum = a_buf[hd, _block(0), :]
        b_cum = b_buf[hd, _block(0), :]
        for k in range(1, PHASES):
            a_k = a_buf[hd, _block(k), :]
            b_cum = a_k * b_cum + b_buf[hd, _block(k), :]
            a_cum = a_k * a_cum
            a_buf[hd, _block(k), :] = a_cum
            b_buf[hd, _block(k), :] = b_cum
        d = 1
        while d < G:
            a_sh = jnp.where(row_g < d, 1.0, pltpu.roll(a_cum, d, axis=0))
            b_sh = jnp.where(row_g < d, 0.0, pltpu.roll(b_cum, d, axis=0))
            b_cum = a_cum * b_sh + b_cum
            a_cum = a_cum * a_sh
            d *= 2
        h0 = h_carry[hd:hd + 1, :]
        h_out = a_cum * h0 + b_cum
        h_carry[hd:hd + 1, :] = h_out[G - 1:G, :]
        h_in = previous_group(h_out, h0)
        for k in range(PHASES):
            h_k = b_buf[hd, _block(k), :] + a_buf[hd, _block(k), :] * h_in
            hg_buf[_block(k), cols] = (h_k * _gelu_tanh(gate_buf[_block(k), cols])).astype(_BF16)

    assert len(FF_CHUNK_STARTS) == 6 and N_HEADS == 8
    ffn_chunk(0)
    h_buf[...] = _rmsnorm(x_buf[x_slot], g_mix_ref[...]).astype(_BF16)
    urnn_buf[...] = _dot(h_buf[...], w_in_ref[:, O_RNN:O_GATE])
    upool_buf[...] = _dot(h_buf[...], w_in_ref[:, 0:O_RNN])
    gate_buf[...] = _dot(h_buf[...], w_in_ref[:, O_GATE:O_GPOOL])
    conv_all_heads()
    pool_windows()
    rnn_head(0)
    ffn_chunk(1)
    rnn_head(1)
    ffn_chunk(2)
    rnn_head(2)
    ffn_chunk(3)
    rnn_head(3)
    ffn_chunk(4)
    rnn_head(4)
    ffn_chunk(5)
    rnn_head(5)
    sg_pool_buf[...] = _sigmoid(_dot(h_buf[...], w_in_ref[:, O_GPOOL:O_GRNN]))
    rnn_head(6)
    pool_group_dots()
    sg_rnn_buf[...] = _sigmoid(_dot(h_buf[...], w_in_ref[:, O_GRNN:O_GRNN + D_MODEL]))
    rnn_head(7)
    y_pool = _dot(mixed_buf[...], w_pool_out_ref[...])
    ffn_out_first_half()
    y_rnn = _dot(hg_buf[...], w_rnn_out_ref[...])
    ffn_out_second_half()
    mix = sg_pool_buf[...] * y_pool + sg_rnn_buf[...] * y_rnn
    x1 = x_buf[x_slot] + _dot(mix.astype(_BF16), w_o_ref[...])
    x1_buf[...] = x1
    h2_buf[...] = _rmsnorm(x1, g_ffn_ref[...]).astype(_BF16)

    @pl.when(j >= 1)
    def _():
        for_phases(y_copy, j - 1, "start")

    @pl.when(j == n_tiles)
    def _():
        for_phases(y_copy, n_tiles - 2, "wait")
        for_phases(y_copy, n_tiles - 1, "wait")


def _resident(shape):
    n = len(shape)
    return pl.BlockSpec(shape, lambda j: (0,) * n, pipeline_mode=pl.Buffered(1))


@jax.jit
def kernel(x, norm_mix, w_in, w_pool_grp, pool_scale, w_pool_out, conv_w, conv_b, w_rg_a, b_rg_a,
           w_rg_x, b_rg_x, lru_lambda, w_rnn_out, w_o, norm_ffn, w_ffn_in, w_ffn_out, norm_final):
    B, S, D = x.shape
    assert D == D_MODEL and S % SEQ_TILE == 0
    assert norm_mix.shape[0] == 1, "single-layer stack"
    T = SEQ_TILE
    nt = S // T
    n_tiles = B * nt
    assert n_tiles >= N_SLOTS

    b_ax = jnp.concatenate([b_rg_a[0], b_rg_x[0]], axis=-1)
    in_hbm = pl.BlockSpec(memory_space=pl.ANY)
    operands_and_specs = [
        (x.reshape(B, S // PHASES, PHASES, D), in_hbm),
        (norm_mix[0].reshape(1, D), None),
        (w_in[0], in_hbm),
        (w_pool_grp[0].reshape(D_POOL, LANE), in_hbm),
        (pool_scale[0].reshape(1, D_POOL), None),
        (w_pool_out[0], in_hbm),
        (conv_w[0], None),
        (conv_b[0].reshape(1, D_RNN), None),
        (w_rg_a[0].reshape(D_RNN, LANE), in_hbm),
        (w_rg_x[0].reshape(D_RNN, LANE), in_hbm),
        (b_ax, None),
        (lru_lambda[0].reshape(1, D_RNN), None),
        (w_rnn_out[0], in_hbm),
        (w_o[0], in_hbm),
        (norm_ffn[0].reshape(1, D), None),
        (w_ffn_in[0], in_hbm),
        (w_ffn_out[0], in_hbm),
        (norm_final.reshape(1, D), None),
    ]
    operands = [op for op, _ in operands_and_specs]
    in_specs = [spec if spec is not None else _resident(op.shape)
                for op, spec in operands_and_specs]
    n_carry_rows = len(POOL_SHIFTS) * PHASES
    d_in = w_in.shape[-1]
    assert d_in == O_GRNN + D_MODEL and d_in <= 2 * D_FF

    out = pl.pallas_call(
        functools.partial(_layer_kernel, nt, n_tiles),
        grid=(n_tiles + 1,),
        in_specs=in_specs,
        out_specs=pl.BlockSpec(memory_space=pl.ANY),
        out_shape=jax.ShapeDtypeStruct((B, S // PHASES, PHASES, D), x.dtype),
        scratch_shapes=[
            pltpu.VMEM((N_SLOTS, T, D), _F32),
            pltpu.VMEM((N_SLOTS, T, D), _F32),
            pltpu.SemaphoreType.DMA((N_SLOTS,)),
            pltpu.SemaphoreType.DMA((N_SLOTS,)),
            pltpu.VMEM((D, d_in), _BF16),
            pltpu.VMEM((D_POOL, LANE), _BF16),
            pltpu.VMEM((D_POOL, D), _BF16),
            pltpu.VMEM((D_RNN, 2 * LANE), _BF16),
            pltpu.VMEM((D_RNN, D), _BF16),
            pltpu.VMEM((D, D), _BF16),
            pltpu.VMEM((D, 2 * D_FF), _BF16),
            pltpu.VMEM((D_FF, D), _BF16),
            pltpu.VMEM((WIDE_STAGE_SLOTS, WIDE_STAGE_ROWS, 2 * D_FF), _F32),
            pltpu.VMEM((NARROW_STAGE_SLOTS, NARROW_STAGE_ROWS, D), _F32),
            pltpu.SemaphoreType.DMA((WIDE_STAGE_SLOTS,)),
            pltpu.SemaphoreType.DMA((NARROW_STAGE_SLOTS,)),
            pltpu.VMEM((T, D_POOL), _F32),
            pltpu.VMEM((T, D_RNN), _F32),
            pltpu.VMEM((T, D_RNN), _F32),
            pltpu.VMEM((T, D_RNN), _F32),
            pltpu.VMEM((T, D_RNN), _BF16),
            pltpu.VMEM((N_HEADS, T, LANE), _F32),
            pltpu.VMEM((N_HEADS, T, LANE), _F32),
            pltpu.VMEM((T, D_RNN), _BF16),
            pltpu.VMEM((T, D_FF), _BF16),
            pltpu.VMEM((T, D), _F32),
            pltpu.VMEM((T, D), _BF16),
            pltpu.VMEM((T, D), _BF16),
            pltpu.VMEM((T, D_POOL), _BF16),
            pltpu.VMEM((T, D_POOL), _BF16),
            pltpu.VMEM((T, D), _F32),
            pltpu.VMEM((T, D), _F32),
            pltpu.VMEM((T, D), _F32),
            pltpu.VMEM((n_carry_rows, D_POOL), _F32),
            pltpu.VMEM((PHASES, D_RNN), _F32),
            pltpu.VMEM((N_HEADS, LANE), _F32),
        ],
        compiler_params=pltpu.CompilerParams(
            dimension_semantics=("arbitrary",),
            vmem_limit_bytes=VMEM_LIMIT_BYTES,
        ),
        name="hybrid_layer",
    )(*operands)
    return out.reshape(B, S, D)
```

```python
import functools
import math

import jax
import jax.numpy as jnp
from jax.experimental import pallas as pl
from jax.experimental.pallas import tpu as pltpu

D_MODEL = 1024
POOL_WINDOWS = (2, 4, 8, 16)
LANE = 128
PHASES = 8
D_POOL = LANE * len(POOL_WINDOWS)
D_RNN = D_MODEL
N_HEADS = D_RNN // LANE
CONV_WIDTH = 4
LRU_C = 8.0
D_FF = 2816
NORM_EPS = 1e-6

SEQ_TILE = 256
GROUPS = SEQ_TILE // PHASES
N_SLOTS = 2
FF_CHUNK = 512
FF_CHUNK_STARTS = tuple(range(0, D_FF, FF_CHUNK))
FF_OUT_SPLIT = 1536
POOL_SHIFTS = (1, 2, 4, 8)
WIDE_STAGE_ROWS = 32
WIDE_STAGE_SLOTS = 6
NARROW_STAGE_ROWS = 256
NARROW_STAGE_SLOTS = 4
VMEM_LIMIT_BYTES = 58 * 1024 * 1024

O_RNN = D_POOL
O_GATE = O_RNN + D_RNN
O_GPOOL = O_GATE + D_RNN
O_GRNN = O_GPOOL + D_MODEL

_F32 = jnp.float32
_BF16 = jnp.bfloat16


def _dot(a, b):
    return jnp.dot(a, b, preferred_element_type=_F32)


def _rmsnorm(x, g):
    return x * jax.lax.rsqrt(jnp.mean(x * x, axis=-1, keepdims=True) + NORM_EPS) * g


def _sigmoid(x):
    return 0.5 * jnp.tanh(0.5 * x) + 0.5


def _gelu_tanh(x):
    c = math.sqrt(2.0 / math.pi)
    return 0.5 * x * (1.0 + jnp.tanh(c * (x + 0.044715 * (x * x * x))))


def _sqrt_nonneg(x):
    return jnp.where(x > 0.0, x * jax.lax.rsqrt(x), 0.0)


def _block(k):
    return slice(k * GROUPS, (k + 1) * GROUPS)


def _load_weights_as_bf16(jobs, stage, sem):
    n_slots = stage.shape[0]

    def copy(i):
        src, r0, rows, _, _ = jobs[i]
        return pltpu.make_async_copy(src.at[pl.ds(r0, rows), :],
                                     stage.at[i % n_slots, pl.ds(0, rows), pl.ds(0, src.shape[1])],
                                     sem.at[i % n_slots])

    for i in range(min(n_slots, len(jobs))):
        copy(i).start()
    for i, (_, r0, rows, dst, col_map) in enumerate(jobs):
        copy(i).wait()
        for src_c0, dst_c0, width in col_map:
            dst[r0:r0 + rows, dst_c0:dst_c0 + width] = (
                stage[i % n_slots, 0:rows, src_c0:src_c0 + width].astype(_BF16))
        if i + n_slots < len(jobs):
            copy(i + n_slots).start()


def _row_chunks(src, dst, rows_per_chunk, col_map=None, dst_first_col=0):
    n_rows, n_cols = src.shape
    assert n_rows % rows_per_chunk == 0
    if col_map is None:
        col_map = [(0, dst_first_col, n_cols)]
    return [(src, r0, rows_per_chunk, dst, col_map) for r0 in range(0, n_rows, rows_per_chunk)]


FFN_IN_COL_MAP = ([(n * LANE, 2 * n * LANE, LANE) for n in range(D_FF // LANE)]
                  + [(D_FF + n * LANE, (2 * n + 1) * LANE, LANE) for n in range(D_FF // LANE)])


def _layer_kernel(tiles_per_seq, n_tiles,
                  x_hbm, g_mix_ref, w_in_hbm, w_grp_hbm, pool_scale_ref, w_pool_out_hbm,
                  conv_w_ref, conv_b_ref, w_rg_a_hbm, w_rg_x_hbm, b_ax_ref, lam_ref,
                  w_rnn_out_hbm, w_o_hbm, g_ffn_ref, w_ffn_in_hbm, w_ffn_out_hbm, g_final_ref,
                  o_hbm,
                  x_buf, y_buf, in_sem, out_sem,
                  w_in_ref, w_grp_ref, w_pool_out_ref, w_ax_ref, w_rnn_out_ref, w_o_ref,
                  w_ffn_in_ref, w_ffn_out_ref, stage_wide, stage_narrow, wide_sem, narrow_sem,
                  upool_buf, urnn_buf, gate_buf, v_buf, vb_buf, a_buf, b_buf, hg_buf, act_buf,
                  x1_buf, h_buf, h2_buf, pooled_buf, mixed_buf, sg_pool_buf, sg_rnn_buf,
                  ffn_acc, pool_carry, conv_carry, h_carry):
    T = SEQ_TILE
    G = GROUPS
    j = pl.program_id(0)
    s = j % tiles_per_seq
    x_slot = j % N_SLOTS
    y_slot = (j + 1) % N_SLOTS

    def tile_phase_hbm(ref, t, k):
        return ref.at[t // tiles_per_seq, pl.ds((t % tiles_per_seq) * G, G), k, :]

    def x_copy(t, k):
        return pltpu.make_async_copy(tile_phase_hbm(x_hbm, t, k),
                                     x_buf.at[t % N_SLOTS, pl.ds(k * G, G), :],
                                     in_sem.at[t % N_SLOTS])

    def y_copy(t, k):
        return pltpu.make_async_copy(y_buf.at[t % N_SLOTS, pl.ds(k * G, G), :],
                                     tile_phase_hbm(o_hbm, t, k),
                                     out_sem.at[t % N_SLOTS])

    def for_phases(copy, t, action):
        for k in range(PHASES):
            getattr(copy(t, k), action)()

    @pl.when(j == 0)
    def _():
        for_phases(x_copy, 0, "start")
        x1_buf[...] = jnp.zeros_like(x1_buf)
        h2_buf[...] = jnp.zeros_like(h2_buf)
        _load_weights_as_bf16(
            _row_chunks(w_in_hbm, w_in_ref, WIDE_STAGE_ROWS)
            + _row_chunks(w_ffn_in_hbm, w_ffn_in_ref, WIDE_STAGE_ROWS, col_map=FFN_IN_COL_MAP),
            stage_wide, wide_sem)
        _load_weights_as_bf16(
            _row_chunks(w_ffn_out_hbm, w_ffn_out_ref, NARROW_STAGE_ROWS)
            + _row_chunks(w_rnn_out_hbm, w_rnn_out_ref, NARROW_STAGE_ROWS)
            + _row_chunks(w_o_hbm, w_o_ref, NARROW_STAGE_ROWS)
            + _row_chunks(w_pool_out_hbm, w_pool_out_ref, NARROW_STAGE_ROWS)
            + _row_chunks(w_rg_a_hbm, w_ax_ref, NARROW_STAGE_ROWS)
            + _row_chunks(w_rg_x_hbm, w_ax_ref, NARROW_STAGE_ROWS, dst_first_col=LANE)
            + _row_chunks(w_grp_hbm, w_grp_ref, NARROW_STAGE_ROWS),
            stage_narrow, narrow_sem)

    @pl.when(j + 1 < n_tiles)
    def _():
        for_phases(x_copy, j + 1, "start")

    @pl.when(j < n_tiles)
    def _():
        for_phases(x_copy, j, "wait")

    @pl.when(j >= N_SLOTS + 1)
    def _():
        for_phases(y_copy, j - 1 - N_SLOTS, "wait")

    @pl.when(s == 0)
    def _():
        pool_carry[...] = jnp.zeros_like(pool_carry)
        conv_carry[...] = jnp.zeros_like(conv_carry)
        h_carry[...] = jnp.zeros_like(h_carry)

    def head_cols(hd):
        return slice(hd * LANE, (hd + 1) * LANE)

    row_g = jax.lax.broadcasted_iota(jnp.int32, (G, LANE), 0)

    def previous_group(blk, last_row_before_tile):
        return jnp.where(row_g == 0, last_row_before_tile, pltpu.roll(blk, 1, axis=0))

    def ffn_chunk(n):
        c = FF_CHUNK_STARTS[n]
        wd = min(FF_CHUNK, D_FF - c)
        gate_up = _dot(h2_buf[...], w_ffn_in_ref[:, 2 * c:2 * (c + wd)])
        for blk in range(wd // LANE):
            gate = gate_up[:, 2 * blk * LANE:(2 * blk + 1) * LANE]
            up = gate_up[:, (2 * blk + 1) * LANE:(2 * blk + 2) * LANE]
            act_buf[:, c + blk * LANE:c + (blk + 1) * LANE] = (
                gate * _sigmoid(gate) * up).astype(_BF16)

    def ffn_out_first_half():
        ffn_acc[...] = x1_buf[...] + _dot(act_buf[:, 0:FF_OUT_SPLIT],
                                          w_ffn_out_ref[0:FF_OUT_SPLIT, :])

    def ffn_out_second_half():
        x2 = ffn_acc[...] + _dot(act_buf[:, FF_OUT_SPLIT:D_FF], w_ffn_out_ref[FF_OUT_SPLIT:D_FF, :])
        y_buf[y_slot] = _rmsnorm(x2, g_final_ref[...])

    def pool_windows():
        first_groups = jax.lax.broadcasted_iota(jnp.int32, (PHASES, LANE), 0)
        for g, w in enumerate(POOL_WINDOWS):
            cols = head_cols(g)
            tok = [upool_buf[_block(k), cols] for k in range(PHASES)]
            level = tok
            for i, d in enumerate(POOL_SHIFTS):
                if d >= w:
                    break
                shifted = []
                for k in range(PHASES):
                    if k >= d:
                        shifted.append(level[k - d])
                    else:
                        src = k - d + PHASES
                        row = i * PHASES + src
                        shifted.append(previous_group(level[src], pool_carry[row:row + 1, cols]))
                for src in range(PHASES - d, PHASES):
                    row = i * PHASES + src
                    pool_carry[row:row + 1, cols] = level[src][G - 1:G, :]
                level = [level[k] + shifted[k] for k in range(PHASES)]
            for k in range(PHASES):
                pos1 = s * T + PHASES * first_groups + (k + 1)
                inv_head = 1.0 / jnp.minimum(pos1, w).astype(_F32)
                inv_count = jnp.concatenate(
                    [inv_head, jnp.full((G - PHASES, LANE), 1.0 / w, _F32)], axis=0)
                pooled = level[k] * inv_count - tok[k]
                pooled_buf[_block(k), cols] = pooled.astype(_BF16)

    def pool_group_dots():
        for g in range(len(POOL_WINDOWS)):
            cols = head_cols(g)
            mixed = _dot(pooled_buf[:, cols], w_grp_ref[cols, :]) * pool_scale_ref[:, cols]
            mixed_buf[:, cols] = mixed.astype(_BF16)

    def conv_all_heads():
        for hd in range(N_HEADS):
            cols = head_cols(hd)
            tok = [urnn_buf[_block(k), cols] for k in range(PHASES)]
            prev = {}
            for src in range(PHASES - CONV_WIDTH + 1, PHASES):
                prev[src] = previous_group(tok[src], conv_carry[src:src + 1, cols])
                conv_carry[src:src + 1, cols] = tok[src][G - 1:G, :]
            for k in range(PHASES):
                v = conv_b_ref[:, cols]
                for tap in range(CONV_WIDTH):
                    m = CONV_WIDTH - 1 - tap
                    u = tok[k - m] if k >= m else prev[k - m + PHASES]
                    v = v + u * conv_w_ref[tap:tap + 1, cols]
                v_buf[_block(k), cols] = v
                vb_buf[_block(k), cols] = v.astype(_BF16)

    lam = lam_ref[...]
    softplus_neg_lam = jnp.maximum(-lam, 0.0) + jnp.log1p(jnp.exp(-jnp.abs(lam)))
    log_a_scale = -LRU_C * softplus_neg_lam

    def rnn_head(hd):
        cols = head_cols(hd)
        ri = _sigmoid(_dot(vb_buf[:, cols], w_ax_ref[cols, :]) + b_ax_ref[hd:hd + 1, :])
        r = ri[:, 0:LANE]
        i = ri[:, LANE:2 * LANE]
        log_a = r * log_a_scale[:, cols]
        a = jnp.exp(log_a)
        b = _sqrt_nonneg(-jnp.tanh(log_a) * (1.0 + a * a)) * i * v_buf[:, cols]
        a_buf[hd] = a
        b_buf[hd] = b

        a_cum = a_buf[hd, _block(0), :]
        b_cum = b_buf[hd, _block(0), :]
        for k in range(1, PHASES):
            a_k = a_buf[hd, _block(k), :]
            b_cum = a_k * b_cum + b_buf[hd, _block(k), :]
            a_cum = a_k * a_cum
            a_buf[hd, _block(k), :] = a_cum
            b_buf[hd, _block(k), :] = b_cum
        d = 1
        while d < G:
            a_sh = jnp.where(row_g < d, 1.0, pltpu.roll(a_cum, d, axis=0))
            b_sh = jnp.where(row_g < d, 0.0, pltpu.roll(b_cum, d, axis=0))
            b_cum = a_cum * b_sh + b_cum
            a_cum = a_cum * a_sh
            d *= 2
        h0 = h_carry[hd:hd + 1, :]
        h_out = a_cum * h0 + b_cum
        h_carry[hd:hd + 1, :] = h_out[G - 1:G, :]
        h_in = previous_group(h_out, h0)
        for k in range(PHASES):
            h_k = b_buf[hd, _block(k), :] + a_buf[hd, _block(k), :] * h_in
            hg_buf[_block(k), cols] = (h_k * _gelu_tanh(gate_buf[_block(k), cols])).astype(_BF16)

    assert len(FF_CHUNK_STARTS) == 6 and N_HEADS == 8
    ffn_chunk(0)
    h_buf[...] = _rmsnorm(x_buf[x_slot], g_mix_ref[...]).astype(_BF16)
    urnn_buf[...] = _dot(h_buf[...], w_in_ref[:, O_RNN:O_GATE])
    upool_buf[...] = _dot(h_buf[...], w_in_ref[:, 0:O_RNN])
    gate_buf[...] = _dot(h_buf[...], w_in_ref[:, O_GATE:O_GPOOL])
    conv_all_heads()
    pool_windows()
    rnn_head(0)
    ffn_chunk(1)
    rnn_head(1)
    ffn_chunk(2)
    rnn_head(2)
    ffn_chunk(3)
    rnn_head(3)
    ffn_chunk(4)
    rnn_head(4)
    ffn_chunk(5)
    rnn_head(5)
    sg_pool_buf[...] = _sigmoid(_dot(h_buf[...], w_in_ref[:, O_GPOOL:O_GRNN]))
    rnn_head(6)
    pool_group_dots()
    sg_rnn_buf[...] = _sigmoid(_dot(h_buf[...], w_in_ref[:, O_GRNN:O_GRNN + D_MODEL]))
    rnn_head(7)
    y_pool = _dot(mixed_buf[...], w_pool_out_ref[...])
    ffn_out_first_half()
    y_rnn = _dot(hg_buf[...], w_rnn_out_ref[...])
    ffn_out_second_half()
    mix = sg_pool_buf[...] * y_pool + sg_rnn_buf[...] * y_rnn
    x1 = x_buf[x_slot] + _dot(mix.astype(_BF16), w_o_ref[...])
    x1_buf[...] = x1
    h2_buf[...] = _rmsnorm(x1, g_ffn_ref[...]).astype(_BF16)

    @pl.when(j >= 1)
    def _():
        for_phases(y_copy, j - 1, "start")

    @pl.when(j == n_tiles)
    def _():
        for_phases(y_copy, n_tiles - 2, "wait")
        for_phases(y_copy, n_tiles - 1, "wait")


def _resident(shape):
    n = len(shape)
    return pl.BlockSpec(shape, lambda j: (0,) * n, pipeline_mode=pl.Buffered(1))


@jax.jit
def kernel(x, norm_mix, w_in, w_pool_grp, pool_scale, w_pool_out, conv_w, conv_b, w_rg_a, b_rg_a,
           w_rg_x, b_rg_x, lru_lambda, w_rnn_out, w_o, norm_ffn, w_ffn_in, w_ffn_out, norm_final):
    B, S, D = x.shape
    assert D == D_MODEL and S % SEQ_TILE == 0
    assert norm_mix.shape[0] == 1, "single-layer stack"
    T = SEQ_TILE
    nt = S // T
    n_tiles = B * nt
    assert n_tiles >= N_SLOTS

    b_ax = jnp.concatenate([b_rg_a[0], b_rg_x[0]], axis=-1)
    in_hbm = pl.BlockSpec(memory_space=pl.ANY)
    operands_and_specs = [
        (x.reshape(B, S // PHASES, PHASES, D), in_hbm),
        (norm_mix[0].reshape(1, D), None),
        (w_in[0], in_hbm),
        (w_pool_grp[0].reshape(D_POOL, LANE), in_hbm),
        (pool_scale[0].reshape(1, D_POOL), None),
        (w_pool_out[0], in_hbm),
        (conv_w[0], None),
        (conv_b[0].reshape(1, D_RNN), None),
        (w_rg_a[0].reshape(D_RNN, LANE), in_hbm),
        (w_rg_x[0].reshape(D_RNN, LANE), in_hbm),
        (b_ax, None),
        (lru_lambda[0].reshape(1, D_RNN), None),
        (w_rnn_out[0], in_hbm),
        (w_o[0], in_hbm),
        (norm_ffn[0].reshape(1, D), None),
        (w_ffn_in[0], in_hbm),
        (w_ffn_out[0], in_hbm),
        (norm_final.reshape(1, D), None),
    ]
    operands = [op for op, _ in operands_and_specs]
    in_specs = [spec if spec is not None else _resident(op.shape)
                for op, spec in operands_and_specs]
    n_carry_rows = len(POOL_SHIFTS) * PHASES
    d_in = w_in.shape[-1]
    assert d_in == O_GRNN + D_MODEL and d_in <= 2 * D_FF

    out = pl.pallas_call(
        functools.partial(_layer_kernel, nt, n_tiles),
        grid=(n_tiles + 1,),
        in_specs=in_specs,
        out_specs=pl.BlockSpec(memory_space=pl.ANY),
        out_shape=jax.ShapeDtypeStruct((B, S // PHASES, PHASES, D), x.dtype),
        scratch_shapes=[
            pltpu.VMEM((N_SLOTS, T, D), _F32),
            pltpu.VMEM((N_SLOTS, T, D), _F32),
            pltpu.SemaphoreType.DMA((N_SLOTS,)),
            pltpu.SemaphoreType.DMA((N_SLOTS,)),
            pltpu.VMEM((D, d_in), _BF16),
            pltpu.VMEM((D_POOL, LANE), _BF16),
            pltpu.VMEM((D_POOL, D), _BF16),
            pltpu.VMEM((D_RNN, 2 * LANE), _BF16),
            pltpu.VMEM((D_RNN, D), _BF16),
            pltpu.VMEM((D, D), _BF16),
            pltpu.VMEM((D, 2 * D_FF), _BF16),
            pltpu.VMEM((D_FF, D), _BF16),
            pltpu.VMEM((WIDE_STAGE_SLOTS, WIDE_STAGE_ROWS, 2 * D_FF), _F32),
            pltpu.VMEM((NARROW_STAGE_SLOTS, NARROW_STAGE_ROWS, D), _F32),
            pltpu.SemaphoreType.DMA((WIDE_STAGE_SLOTS,)),
            pltpu.SemaphoreType.DMA((NARROW_STAGE_SLOTS,)),
            pltpu.VMEM((T, D_POOL), _F32),
            pltpu.VMEM((T, D_RNN), _F32),
            pltpu.VMEM((T, D_RNN), _F32),
            pltpu.VMEM((T, D_RNN), _F32),
            pltpu.VMEM((T, D_RNN), _BF16),
            pltpu.VMEM((N_HEADS, T, LANE), _F32),
            pltpu.VMEM((N_HEADS, T, LANE), _F32),
            pltpu.VMEM((T, D_RNN), _BF16),
            pltpu.VMEM((T, D_FF), _BF16),
            pltpu.VMEM((T, D), _F32),
            pltpu.VMEM((T, D), _BF16),
            pltpu.VMEM((T, D), _BF16),
            pltpu.VMEM((T, D_POOL), _BF16),
            pltpu.VMEM((T, D_POOL), _BF16),
            pltpu.VMEM((T, D), _F32),
            pltpu.VMEM((T, D), _F32),
            pltpu.VMEM((T, D), _F32),
            pltpu.VMEM((n_carry_rows, D_POOL), _F32),
            pltpu.VMEM((PHASES, D_RNN), _F32),
            pltpu.VMEM((N_HEADS, LANE), _F32),
        ],
        compiler_params=pltpu.CompilerParams(
            dimension_semantics=("arbitrary",),
            vmem_limit_bytes=VMEM_LIMIT_BYTES,
        ),
        name="hybrid_layer",
    )(*operands)
    return out.reshape(B, S, D)
```

```python
import functools
import math

import jax
import jax.numpy as jnp
from jax.experimental import pallas as pl
from jax.experimental.pallas import tpu as pltpu

D_MODEL = 1024
POOL_WINDOWS = (2, 4, 8, 16)
LANE = 128
PHASES = 8
D_POOL = LANE * len(POOL_WINDOWS)
D_RNN = D_MODEL
N_HEADS = D_RNN // LANE
CONV_WIDTH = 4
LRU_C = 8.0
D_FF = 2816
NORM_EPS = 1e-6

SEQ_TILE = 256
GROUPS = SEQ_TILE // PHASES
N_SLOTS = 2
FF_CHUNK = 512
FF_CHUNK_STARTS = tuple(range(0, D_FF, FF_CHUNK))
FF_OUT_SPLIT = 1536
POOL_SHIFTS = (1, 2, 4, 8)
WIDE_STAGE_ROWS = 32
WIDE_STAGE_SLOTS = 6
NARROW_STAGE_ROWS = 256
NARROW_STAGE_SLOTS = 4
VMEM_LIMIT_BYTES = 58 * 1024 * 1024

O_RNN = D_POOL
O_GATE = O_RNN + D_RNN
O_GPOOL = O_GATE + D_RNN
O_GRNN = O_GPOOL + D_MODEL

_F32 = jnp.float32
_BF16 = jnp.bfloat16


def _dot(a, b):
    return jnp.dot(a, b, preferred_element_type=_F32)


def _rmsnorm(x, g):
    return x * jax.lax.rsqrt(jnp.mean(x * x, axis=-1, keepdims=True) + NORM_EPS) * g


def _sigmoid_of_twice(z):
    return 0.5 * jnp.tanh(z) + 0.5


def _gelu_tanh(x):
    c = math.sqrt(2.0 / math.pi)
    return 0.5 * x * (1.0 + jnp.tanh(c * (x + 0.044715 * (x * x * x))))


def _sqrt_nonneg(x):
    return jnp.where(x > 0.0, x * jax.lax.rsqrt(x), 0.0)


def _block(k):
    return slice(k * GROUPS, (k + 1) * GROUPS)


def _load_weights_as_bf16(jobs, stage, sem):
    n_slots = stage.shape[0]

    def copy(i):
        src, r0, rows, _, _ = jobs[i]
        return pltpu.make_async_copy(src.at[pl.ds(r0, rows), :],
                                     stage.at[i % n_slots, pl.ds(0, rows), pl.ds(0, src.shape[1])],
                                     sem.at[i % n_slots])

    for i in range(min(n_slots, len(jobs))):
        copy(i).start()
    for i, (_, r0, rows, dst, col_map) in enumerate(jobs):
        copy(i).wait()
        for src_c0, dst_c0, width, scale in col_map:
            w = stage[i % n_slots, 0:rows, src_c0:src_c0 + width]
            if scale != 1.0:
                assert math.frexp(scale)[0] == 0.5, "only power-of-two scales are exact"
                w = w * scale
            dst[r0:r0 + rows, dst_c0:dst_c0 + width] = w.astype(_BF16)
        if i + n_slots < len(jobs):
            copy(i + n_slots).start()


def _row_chunks(src, dst, rows_per_chunk, col_map=None, dst_first_col=0, scale=1.0):
    n_rows, n_cols = src.shape
    assert n_rows % rows_per_chunk == 0
    if col_map is None:
        col_map = [(0, dst_first_col, n_cols, scale)]
    return [(src, r0, rows_per_chunk, dst, col_map) for r0 in range(0, n_rows, rows_per_chunk)]


SIGMOID_INPUT_SCALE = 0.5
FFN_IN_COL_MAP = (
    [(n * LANE, 2 * n * LANE, LANE, SIGMOID_INPUT_SCALE) for n in range(D_FF // LANE)]
    + [(D_FF + n * LANE, (2 * n + 1) * LANE, LANE, 1.0) for n in range(D_FF // LANE)])
W_IN_COL_MAP = [(0, 0, O_GPOOL, 1.0), (O_GPOOL, O_GPOOL, 2 * D_MODEL, SIGMOID_INPUT_SCALE)]


def _layer_kernel(tiles_per_seq, n_tiles,
                  x_hbm, g_mix_ref, w_in_hbm, w_grp_hbm, pool_scale_ref, w_pool_out_hbm,
                  conv_w_ref, conv_b_ref, w_rg_a_hbm, w_rg_x_hbm, b_ax_ref, lam_ref,
                  w_rnn_out_hbm, w_o_hbm, g_ffn_ref, w_ffn_in_hbm, w_ffn_out_hbm, g_final_ref,
                  o_hbm,
                  x_buf, y_buf, in_sem, out_sem,
                  w_in_ref, w_grp_ref, w_pool_out_ref, w_ax_ref, w_rnn_out_ref, w_o_ref,
                  w_ffn_in_ref, w_ffn_out_ref, stage_wide, stage_narrow, wide_sem, narrow_sem,
                  upool_buf, urnn_buf, gate_buf, v_buf, vb_buf, a_buf, b_buf, hg_buf, act_buf,
                  x1_buf, h_buf, h2_buf, pooled_buf, mixed_buf, sg_pool_buf, sg_rnn_buf,
                  ffn_acc, pool_carry, conv_carry, h_carry):
    T = SEQ_TILE
    G = GROUPS
    j = pl.program_id(0)
    s = j % tiles_per_seq
    x_slot = j % N_SLOTS
    y_slot = (j + 1) % N_SLOTS

    def tile_phase_hbm(ref, t, k):
        return ref.at[t // tiles_per_seq, pl.ds((t % tiles_per_seq) * G, G), k, :]

    def x_copy(t, k):
        return pltpu.make_async_copy(tile_phase_hbm(x_hbm, t, k),
                                     x_buf.at[t % N_SLOTS, pl.ds(k * G, G), :],
                                     in_sem.at[t % N_SLOTS])

    def y_copy(t, k):
        return pltpu.make_async_copy(y_buf.at[t % N_SLOTS, pl.ds(k * G, G), :],
                                     tile_phase_hbm(o_hbm, t, k),
                                     out_sem.at[t % N_SLOTS])

    def for_phases(copy, t, action):
        for k in range(PHASES):
            getattr(copy(t, k), action)()

    @pl.when(j == 0)
    def _():
        for_phases(x_copy, 0, "start")
        x1_buf[...] = jnp.zeros_like(x1_buf)
        h2_buf[...] = jnp.zeros_like(h2_buf)
        _load_weights_as_bf16(
            _row_chunks(w_in_hbm, w_in_ref, WIDE_STAGE_ROWS, col_map=W_IN_COL_MAP)
            + _row_chunks(w_ffn_in_hbm, w_ffn_in_ref, WIDE_STAGE_ROWS, col_map=FFN_IN_COL_MAP),
            stage_wide, wide_sem)
        _load_weights_as_bf16(
            _row_chunks(w_ffn_out_hbm, w_ffn_out_ref, NARROW_STAGE_ROWS)
            + _row_chunks(w_rnn_out_hbm, w_rnn_out_ref, NARROW_STAGE_ROWS)
            + _row_chunks(w_o_hbm, w_o_ref, NARROW_STAGE_ROWS)
            + _row_chunks(w_pool_out_hbm, w_pool_out_ref, NARROW_STAGE_ROWS)
            + _row_chunks(w_rg_a_hbm, w_ax_ref, NARROW_STAGE_ROWS, scale=SIGMOID_INPUT_SCALE)
            + _row_chunks(w_rg_x_hbm, w_ax_ref, NARROW_STAGE_ROWS, dst_first_col=LANE,
                          scale=SIGMOID_INPUT_SCALE)
            + _row_chunks(w_grp_hbm, w_grp_ref, NARROW_STAGE_ROWS),
            stage_narrow, narrow_sem)

    @pl.when(j + 1 < n_tiles)
    def _():
        for_phases(x_copy, j + 1, "start")

    @pl.when(j < n_tiles)
    def _():
        for_phases(x_copy, j, "wait")

    @pl.when(j >= N_SLOTS + 1)
    def _():
        for_phases(y_copy, j - 1 - N_SLOTS, "wait")

    @pl.when(s == 0)
    def _():
        pool_carry[...] = jnp.zeros_like(pool_carry)
        conv_carry[...] = jnp.zeros_like(conv_carry)
        h_carry[...] = jnp.zeros_like(h_carry)

    def head_cols(hd):
        return slice(hd * LANE, (hd + 1) * LANE)

    row_g = jax.lax.broadcasted_iota(jnp.int32, (G, LANE), 0)

    def previous_group(blk, last_row_before_tile):
        return jnp.where(row_g == 0, last_row_before_tile, pltpu.roll(blk, 1, axis=0))

    def ffn_chunk(n):
        c = FF_CHUNK_STARTS[n]
        wd = min(FF_CHUNK, D_FF - c)
        gate_up = _dot(h2_buf[...], w_ffn_in_ref[:, 2 * c:2 * (c + wd)])
        for blk in range(wd // LANE):
            half_gate = gate_up[:, 2 * blk * LANE:(2 * blk + 1) * LANE]
            up = gate_up[:, (2 * blk + 1) * LANE:(2 * blk + 2) * LANE]
            act_buf[:, c + blk * LANE:c + (blk + 1) * LANE] = (
                half_gate * (jnp.tanh(half_gate) + 1.0) * up).astype(_BF16)

    def ffn_out_first_half():
        ffn_acc[...] = x1_buf[...] + _dot(act_buf[:, 0:FF_OUT_SPLIT],
                                          w_ffn_out_ref[0:FF_OUT_SPLIT, :])

    def ffn_out_second_half():
        x2 = ffn_acc[...] + _dot(act_buf[:, FF_OUT_SPLIT:D_FF], w_ffn_out_ref[FF_OUT_SPLIT:D_FF, :])
        y_buf[y_slot] = _rmsnorm(x2, g_final_ref[...])

    def pool_windows():
        first_groups = jax.lax.broadcasted_iota(jnp.int32, (PHASES, LANE), 0)
        for g, w in enumerate(POOL_WINDOWS):
            cols = head_cols(g)
            tok = [upool_buf[_block(k), cols] for k in range(PHASES)]
            level = tok
            for i, d in enumerate(POOL_SHIFTS):
                if d >= w:
                    break
                shifted = []
                for k in range(PHASES):
                    if k >= d:
                        shifted.append(level[k - d])
                    else:
                        src = k - d + PHASES
                        row = i * PHASES + src
                        shifted.append(previous_group(level[src], pool_carry[row:row + 1, cols]))
                for src in range(PHASES - d, PHASES):
                    row = i * PHASES + src
                    pool_carry[row:row + 1, cols] = level[src][G - 1:G, :]
                level = [level[k] + shifted[k] for k in range(PHASES)]
            for k in range(PHASES):
                pos1 = s * T + PHASES * first_groups + (k + 1)
                inv_head = 1.0 / jnp.minimum(pos1, w).astype(_F32)
                inv_count = jnp.concatenate(
                    [inv_head, jnp.full((G - PHASES, LANE), 1.0 / w, _F32)], axis=0)
                pooled = level[k] * inv_count - tok[k]
                pooled_buf[_block(k), cols] = pooled.astype(_BF16)

    def pool_group_dots():
        for g in range(len(POOL_WINDOWS)):
            cols = head_cols(g)
            mixed = _dot(pooled_buf[:, cols], w_grp_ref[cols, :]) * pool_scale_ref[:, cols]
            mixed_buf[:, cols] = mixed.astype(_BF16)

    def conv_all_heads():
        for hd in range(N_HEADS):
            cols = head_cols(hd)
            tok = [urnn_buf[_block(k), cols] for k in range(PHASES)]
            prev = {}
            for src in range(PHASES - CONV_WIDTH + 1, PHASES):
                prev[src] = previous_group(tok[src], conv_carry[src:src + 1, cols])
                conv_carry[src:src + 1, cols] = tok[src][G - 1:G, :]
            for k in range(PHASES):
                v = conv_b_ref[:, cols]
                for tap in range(CONV_WIDTH):
                    m = CONV_WIDTH - 1 - tap
                    u = tok[k - m] if k >= m else prev[k - m + PHASES]
                    v = v + u * conv_w_ref[tap:tap + 1, cols]
                v_buf[_block(k), cols] = v
                vb_buf[_block(k), cols] = v.astype(_BF16)

    lam = lam_ref[...]
    softplus_neg_lam = jnp.maximum(-lam, 0.0) + jnp.log1p(jnp.exp(-jnp.abs(lam)))
    log_a_scale = -LRU_C * softplus_neg_lam

    def rnn_head(hd):
        cols = head_cols(hd)
        ri = _sigmoid_of_twice(_dot(vb_buf[:, cols], w_ax_ref[cols, :])
                               + SIGMOID_INPUT_SCALE * b_ax_ref[hd:hd + 1, :])
        r = ri[:, 0:LANE]
        i = ri[:, LANE:2 * LANE]
        log_a = r * log_a_scale[:, cols]
        a = jnp.exp(log_a)
        b = _sqrt_nonneg(-jnp.tanh(log_a) * (1.0 + a * a)) * i * v_buf[:, cols]
        a_buf[hd] = a
        b_buf[hd] = b

        a_cum = a_buf[hd, _block(0), :]
        b_cum = b_buf[hd, _block(0), :]
        for k in range(1, PHASES):
            a_k = a_buf[hd, _block(k), :]
            b_cum = a_k * b_cum + b_buf[hd, _block(k), :]
            a_cum = a_k * a_cum
        d = 1
        while d < G:
            a_sh = jnp.where(row_g < d, 1.0, pltpu.roll(a_cum, d, axis=0))
            b_sh = jnp.where(row_g < d, 0.0, pltpu.roll(b_cum, d, axis=0))
            b_cum = a_cum * b_sh + b_cum
            a_cum = a_cum * a_sh
            d *= 2
        h0 = h_carry[hd:hd + 1, :]
        h_out = a_cum * h0 + b_cum
        h_carry[hd:hd + 1, :] = h_out[G - 1:G, :]
        h_in = previous_group(h_out, h0)
        h_k = h_in
        for k in range(PHASES):
            h_k = a_buf[hd, _block(k), :] * h_k + b_buf[hd, _block(k), :]
            hg_buf[_block(k), cols] = (h_k * _gelu_tanh(gate_buf[_block(k), cols])).astype(_BF16)

    assert len(FF_CHUNK_STARTS) == 6 and N_HEADS == 8
    ffn_chunk(0)
    h_buf[...] = _rmsnorm(x_buf[x_slot], g_mix_ref[...]).astype(_BF16)
    urnn_buf[...] = _dot(h_buf[...], w_in_ref[:, O_RNN:O_GATE])
    upool_buf[...] = _dot(h_buf[...], w_in_ref[:, 0:O_RNN])
    gate_buf[...] = _dot(h_buf[...], w_in_ref[:, O_GATE:O_GPOOL])
    conv_all_heads()
    pool_windows()
    rnn_head(0)
    ffn_chunk(1)
    rnn_head(1)
    ffn_chunk(2)
    rnn_head(2)
    ffn_chunk(3)
    rnn_head(3)
    ffn_chunk(4)
    rnn_head(4)
    ffn_chunk(5)
    rnn_head(5)
    sg_pool_buf[...] = _sigmoid_of_twice(_dot(h_buf[...], w_in_ref[:, O_GPOOL:O_GRNN]))
    rnn_head(6)
    pool_group_dots()
    sg_rnn_buf[...] = _sigmoid_of_twice(_dot(h_buf[...], w_in_ref[:, O_GRNN:O_GRNN + D_MODEL]))
    rnn_head(7)
    y_pool = _dot(mixed_buf[...], w_pool_out_ref[...])
    ffn_out_first_half()
    y_rnn = _dot(hg_buf[...], w_rnn_out_ref[...])
    ffn_out_second_half()
    mix = sg_pool_buf[...] * y_pool + sg_rnn_buf[...] * y_rnn
    x1 = x_buf[x_slot] + _dot(mix.astype(_BF16), w_o_ref[...])
    x1_buf[...] = x1
    h2_buf[...] = _rmsnorm(x1, g_ffn_ref[...]).astype(_BF16)

    @pl.when(j >= 1)
    def _():
        for_phases(y_copy, j - 1, "start")

    @pl.when(j == n_tiles)
    def _():
        for_phases(y_copy, n_tiles - 2, "wait")
        for_phases(y_copy, n_tiles - 1, "wait")


def _resident(shape):
    n = len(shape)
    return pl.BlockSpec(shape, lambda j: (0,) * n, pipeline_mode=pl.Buffered(1))


@jax.jit
def kernel(x, norm_mix, w_in, w_pool_grp, pool_scale, w_pool_out, conv_w, conv_b, w_rg_a, b_rg_a,
           w_rg_x, b_rg_x, lru_lambda, w_rnn_out, w_o, norm_ffn, w_ffn_in, w_ffn_out, norm_final):
    B, S, D = x.shape
    assert D == D_MODEL and S % SEQ_TILE == 0
    assert norm_mix.shape[0] == 1, "single-layer stack"
    T = SEQ_TILE
    nt = S // T
    n_tiles = B * nt
    assert n_tiles >= N_SLOTS

    b_ax = jnp.concatenate([b_rg_a[0], b_rg_x[0]], axis=-1)
    in_hbm = pl.BlockSpec(memory_space=pl.ANY)
    operands_and_specs = [
        (x.reshape(B, S // PHASES, PHASES, D), in_hbm),
        (norm_mix[0].reshape(1, D), None),
        (w_in[0], in_hbm),
        (w_pool_grp[0].reshape(D_POOL, LANE), in_hbm),
        (pool_scale[0].reshape(1, D_POOL), None),
        (w_pool_out[0], in_hbm),
        (conv_w[0], None),
        (conv_b[0].reshape(1, D_RNN), None),
        (w_rg_a[0].reshape(D_RNN, LANE), in_hbm),
        (w_rg_x[0].reshape(D_RNN, LANE), in_hbm),
        (b_ax, None),
        (lru_lambda[0].reshape(1, D_RNN), None),
        (w_rnn_out[0], in_hbm),
        (w_o[0], in_hbm),
        (norm_ffn[0].reshape(1, D), None),
        (w_ffn_in[0], in_hbm),
        (w_ffn_out[0], in_hbm),
        (norm_final.reshape(1, D), None),
    ]
    operands = [op for op, _ in operands_and_specs]
    in_specs = [spec if spec is not None else _resident(op.shape)
                for op, spec in operands_and_specs]
    n_carry_rows = len(POOL_SHIFTS) * PHASES
    d_in = w_in.shape[-1]
    assert d_in == O_GRNN + D_MODEL and d_in <= 2 * D_FF

    out = pl.pallas_call(
        functools.partial(_layer_kernel, nt, n_tiles),
        grid=(n_tiles + 1,),
        in_specs=in_specs,
        out_specs=pl.BlockSpec(memory_space=pl.ANY),
        out_shape=jax.ShapeDtypeStruct((B, S // PHASES, PHASES, D), x.dtype),
        scratch_shapes=[
            pltpu.VMEM((N_SLOTS, T, D), _F32),
            pltpu.VMEM((N_SLOTS, T, D), _F32),
            pltpu.SemaphoreType.DMA((N_SLOTS,)),
            pltpu.SemaphoreType.DMA((N_SLOTS,)),
            pltpu.VMEM((D, d_in), _BF16),
            pltpu.VMEM((D_POOL, LANE), _BF16),
            pltpu.VMEM((D_POOL, D), _BF16),
            pltpu.VMEM((D_RNN, 2 * LANE), _BF16),
            pltpu.VMEM((D_RNN, D), _BF16),
            pltpu.VMEM((D, D), _BF16),
            pltpu.VMEM((D, 2 * D_FF), _BF16),
            pltpu.VMEM((D_FF, D), _BF16),
            pltpu.VMEM((WIDE_STAGE_SLOTS, WIDE_STAGE_ROWS, 2 * D_FF), _F32),
            pltpu.VMEM((NARROW_STAGE_SLOTS, NARROW_STAGE_ROWS, D), _F32),
            pltpu.SemaphoreType.DMA((WIDE_STAGE_SLOTS,)),
            pltpu.SemaphoreType.DMA((NARROW_STAGE_SLOTS,)),
            pltpu.VMEM((T, D_POOL), _F32),
            pltpu.VMEM((T, D_RNN), _F32),
            pltpu.VMEM((T, D_RNN), _F32),
            pltpu.VMEM((T, D_RNN), _F32),
            pltpu.VMEM((T, D_RNN), _BF16),
            pltpu.VMEM((N_HEADS, T, LANE), _F32),
            pltpu.VMEM((N_HEADS, T, LANE), _F32),
            pltpu.VMEM((T, D_RNN), _BF16),
            pltpu.VMEM((T, D_FF), _BF16),
            pltpu.VMEM((T, D), _F32),
            pltpu.VMEM((T, D), _BF16),
            pltpu.VMEM((T, D), _BF16),
            pltpu.VMEM((T, D_POOL), _BF16),
            pltpu.VMEM((T, D_POOL), _BF16),
            pltpu.VMEM((T, D), _F32),
            pltpu.VMEM((T, D), _F32),
            pltpu.VMEM((T, D), _F32),
            pltpu.VMEM((n_carry_rows, D_POOL), _F32),
            pltpu.VMEM((PHASES, D_RNN), _F32),
            pltpu.VMEM((N_HEADS, LANE), _F32),
        ],
        compiler_params=pltpu.CompilerParams(
            dimension_semantics=("arbitrary",),
            vmem_limit_bytes=VMEM_LIMIT_BYTES,
        ),
        name="hybrid_layer",
    )(*operands)
    return out.reshape(B, S, D)
```

```python
import functools
import math

import jax
import jax.numpy as jnp
from jax.experimental import pallas as pl
from jax.experimental.pallas import tpu as pltpu

D_MODEL = 1024
POOL_WINDOWS = (2, 4, 8, 16)
LANE = 128
PHASES = 8
D_POOL = LANE * len(POOL_WINDOWS)
D_RNN = D_MODEL
N_HEADS = D_RNN // LANE
CONV_WIDTH = 4
LRU_C = 8.0
D_FF = 2816
NORM_EPS = 1e-6

SEQ_TILE = 256
GROUPS = SEQ_TILE // PHASES
N_SLOTS = 2
FF_CHUNK = 512
FF_CHUNK_STARTS = tuple(range(0, D_FF, FF_CHUNK))
FF_OUT_SPLIT = 1536
POOL_SHIFTS = (1, 2, 4, 8)
WIDE_STAGE_ROWS = 32
WIDE_STAGE_SLOTS = 6
NARROW_STAGE_ROWS = 256
NARROW_STAGE_SLOTS = 4
VMEM_LIMIT_BYTES = 60 * 1024 * 1024

O_RNN = D_POOL
O_GATE = O_RNN + D_RNN
O_GPOOL = O_GATE + D_RNN
O_GRNN = O_GPOOL + D_MODEL

_F32 = jnp.float32
_BF16 = jnp.bfloat16


def _dot(a, b):
    return jnp.dot(a, b, preferred_element_type=_F32)


def _rmsnorm(x, g):
    return x * jax.lax.rsqrt(jnp.mean(x * x, axis=-1, keepdims=True) + NORM_EPS) * g


def _sigmoid_of_twice(z):
    return 0.5 * jnp.tanh(z) + 0.5


def _gelu_tanh(x):
    c = math.sqrt(2.0 / math.pi)
    return 0.5 * x * (1.0 + jnp.tanh(c * (x + 0.044715 * (x * x * x))))


def _sqrt_nonneg(x):
    return jnp.where(x > 0.0, x * jax.lax.rsqrt(x), 0.0)


def _block(k):
    return slice(k * GROUPS, (k + 1) * GROUPS)


def _load_weights_as_bf16(jobs, stage, sem):
    n_slots = stage.shape[0]

    def copy(i):
        src, r0, rows, _, _ = jobs[i]
        return pltpu.make_async_copy(src.at[pl.ds(r0, rows), :],
                                     stage.at[i % n_slots, pl.ds(0, rows), pl.ds(0, src.shape[1])],
                                     sem.at[i % n_slots])

    for i in range(min(n_slots, len(jobs))):
        copy(i).start()
    for i, (_, r0, rows, dst, col_map) in enumerate(jobs):
        copy(i).wait()
        for src_c0, dst_c0, width, scale in col_map:
            w = stage[i % n_slots, 0:rows, src_c0:src_c0 + width]
            if scale != 1.0:
                assert math.frexp(scale)[0] == 0.5, "only power-of-two scales are exact"
                w = w * scale
            dst[r0:r0 + rows, dst_c0:dst_c0 + width] = w.astype(_BF16)
        if i + n_slots < len(jobs):
            copy(i + n_slots).start()


def _row_chunks(src, dst, rows_per_chunk, col_map=None, dst_first_col=0, scale=1.0):
    n_rows, n_cols = src.shape
    assert n_rows % rows_per_chunk == 0
    if col_map is None:
        col_map = [(0, dst_first_col, n_cols, scale)]
    return [(src, r0, rows_per_chunk, dst, col_map) for r0 in range(0, n_rows, rows_per_chunk)]


SIGMOID_INPUT_SCALE = 0.5
FFN_IN_COL_MAP = (
    [(n * LANE, 2 * n * LANE, LANE, SIGMOID_INPUT_SCALE) for n in range(D_FF // LANE)]
    + [(D_FF + n * LANE, (2 * n + 1) * LANE, LANE, 1.0) for n in range(D_FF // LANE)])
W_IN_COL_MAP = [(0, 0, O_GPOOL, 1.0), (O_GPOOL, O_GPOOL, 2 * D_MODEL, SIGMOID_INPUT_SCALE)]


def _layer_kernel(tiles_per_seq, n_tiles,
                  x_hbm, g_mix_ref, w_in_hbm, w_grp_hbm, pool_scale_ref, w_pool_out_hbm,
                  conv_w_ref, conv_b_ref, w_rg_a_hbm, w_rg_x_hbm, b_ax_ref, lam_ref,
                  w_rnn_out_hbm, w_o_hbm, g_ffn_ref, w_ffn_in_hbm, w_ffn_out_hbm, g_final_ref,
                  o_hbm,
                  x_buf, y_buf, in_sem, out_sem,
                  w_in_ref, w_grp_ref, w_pool_out_ref, w_ax_ref, w_rnn_out_ref, w_o_ref,
                  w_ffn_in_ref, w_ffn_out_ref, stage_wide, stage_narrow, wide_sem, narrow_sem,
                  upool_buf, urnn_buf, gate_buf, v_buf, vb_buf, a_buf, b_buf, hg_buf, act_buf,
                  x1_buf, h_buf, h2_buf, pooled_buf, mixed_buf, sg_pool_buf, sg_rnn_buf,
                  ffn_acc, pool_carry, conv_carry, h_carry):
    T = SEQ_TILE
    G = GROUPS
    j = pl.program_id(0)
    s = j % tiles_per_seq
    x_slot = j % N_SLOTS
    y_slot = (j + 1) % N_SLOTS

    def tile_phase_hbm(ref, t, k):
        return ref.at[t // tiles_per_seq, pl.ds((t % tiles_per_seq) * G, G), k, :]

    def x_copy(t, k):
        return pltpu.make_async_copy(tile_phase_hbm(x_hbm, t, k),
                                     x_buf.at[t % N_SLOTS, pl.ds(k * G, G), :],
                                     in_sem.at[t % N_SLOTS])

    def y_copy(t, k):
        return pltpu.make_async_copy(y_buf.at[t % N_SLOTS, pl.ds(k * G, G), :],
                                     tile_phase_hbm(o_hbm, t, k),
                                     out_sem.at[t % N_SLOTS])

    def for_phases(copy, t, action):
        for k in range(PHASES):
            getattr(copy(t, k), action)()

    @pl.when(j == 0)
    def _():
        for_phases(x_copy, 0, "start")
        _load_weights_as_bf16(
            _row_chunks(w_in_hbm, w_in_ref, WIDE_STAGE_ROWS, col_map=W_IN_COL_MAP)
            + _row_chunks(w_ffn_in_hbm, w_ffn_in_ref, WIDE_STAGE_ROWS, col_map=FFN_IN_COL_MAP),
            stage_wide, wide_sem)
        _load_weights_as_bf16(
            _row_chunks(w_ffn_out_hbm, w_ffn_out_ref, NARROW_STAGE_ROWS)
            + _row_chunks(w_rnn_out_hbm, w_rnn_out_ref, NARROW_STAGE_ROWS)
            + _row_chunks(w_o_hbm, w_o_ref, NARROW_STAGE_ROWS)
            + _row_chunks(w_pool_out_hbm, w_pool_out_ref, NARROW_STAGE_ROWS)
            + _row_chunks(w_rg_a_hbm, w_ax_ref, NARROW_STAGE_ROWS, scale=SIGMOID_INPUT_SCALE)
            + _row_chunks(w_rg_x_hbm, w_ax_ref, NARROW_STAGE_ROWS, dst_first_col=LANE,
                          scale=SIGMOID_INPUT_SCALE)
            + _row_chunks(w_grp_hbm, w_grp_ref, NARROW_STAGE_ROWS),
            stage_narrow, narrow_sem)

    @pl.when(j + 1 < n_tiles)
    def _():
        for_phases(x_copy, j + 1, "start")

    @pl.when(j < n_tiles)
    def _():
        for_phases(x_copy, j, "wait")

    @pl.when(j >= N_SLOTS + 1)
    def _():
        for_phases(y_copy, j - 1 - N_SLOTS, "wait")

    @pl.when(s == 0)
    def _():
        pool_carry[...] = jnp.zeros_like(pool_carry)
        conv_carry[...] = jnp.zeros_like(conv_carry)
        h_carry[...] = jnp.zeros_like(h_carry)

    def head_cols(hd):
        return slice(hd * LANE, (hd + 1) * LANE)

    row_g = jax.lax.broadcasted_iota(jnp.int32, (G, LANE), 0)

    def previous_group(blk, last_row_before_tile):
        return jnp.where(row_g == 0, last_row_before_tile, pltpu.roll(blk, 1, axis=0))

    def ffn_chunk(n):
        c = FF_CHUNK_STARTS[n]
        wd = min(FF_CHUNK, D_FF - c)
        gate_up = _dot(h2_buf[...], w_ffn_in_ref[:, 2 * c:2 * (c + wd)])
        for blk in range(wd // LANE):
            half_gate = gate_up[:, 2 * blk * LANE:(2 * blk + 1) * LANE]
            up = gate_up[:, (2 * blk + 1) * LANE:(2 * blk + 2) * LANE]
            act_buf[:, c + blk * LANE:c + (blk + 1) * LANE] = (
                half_gate * (jnp.tanh(half_gate) + 1.0) * up).astype(_BF16)

    def ffn_out_first_half():
        ffn_acc[...] = x1_buf[...] + _dot(act_buf[:, 0:FF_OUT_SPLIT],
                                          w_ffn_out_ref[0:FF_OUT_SPLIT, :])

    def ffn_out_second_half():
        x2 = ffn_acc[...] + _dot(act_buf[:, FF_OUT_SPLIT:D_FF], w_ffn_out_ref[FF_OUT_SPLIT:D_FF, :])
        y_buf[y_slot] = _rmsnorm(x2, g_final_ref[...])

    def pool_windows():
        first_groups = jax.lax.broadcasted_iota(jnp.int32, (PHASES, LANE), 0)
        for g, w in enumerate(POOL_WINDOWS):
            cols = head_cols(g)
            tok = [upool_buf[_block(k), cols] for k in range(PHASES)]
            level = tok
            for i, d in enumerate(POOL_SHIFTS):
                if d >= w:
                    break
                shifted = []
                for k in range(PHASES):
                    if k >= d:
                        shifted.append(level[k - d])
                    else:
                        src = k - d + PHASES
                        row = i * PHASES + src
                        shifted.append(previous_group(level[src], pool_carry[row:row + 1, cols]))
                for src in range(PHASES - d, PHASES):
                    row = i * PHASES + src
                    pool_carry[row:row + 1, cols] = level[src][G - 1:G, :]
                level = [level[k] + shifted[k] for k in range(PHASES)]
            for k in range(PHASES):
                pos1 = s * T + PHASES * first_groups + (k + 1)
                inv_head = 1.0 / jnp.minimum(pos1, w).astype(_F32)
                inv_count = jnp.concatenate(
                    [inv_head, jnp.full((G - PHASES, LANE), 1.0 / w, _F32)], axis=0)
                pooled = level[k] * inv_count - tok[k]
                pooled_buf[_block(k), cols] = pooled.astype(_BF16)

    def pool_group_dots():
        for g in range(len(POOL_WINDOWS)):
            cols = head_cols(g)
            mixed = _dot(pooled_buf[:, cols], w_grp_ref[cols, :]) * pool_scale_ref[:, cols]
            mixed_buf[:, cols] = mixed.astype(_BF16)

    def conv_all_heads():
        for hd in range(N_HEADS):
            cols = head_cols(hd)
            tok = [urnn_buf[_block(k), cols] for k in range(PHASES)]
            prev = {}
            for src in range(PHASES - CONV_WIDTH + 1, PHASES):
                prev[src] = previous_group(tok[src], conv_carry[src:src + 1, cols])
                conv_carry[src:src + 1, cols] = tok[src][G - 1:G, :]
            for k in range(PHASES):
                v = conv_b_ref[:, cols]
                for tap in range(CONV_WIDTH):
                    m = CONV_WIDTH - 1 - tap
                    u = tok[k - m] if k >= m else prev[k - m + PHASES]
                    v = v + u * conv_w_ref[tap:tap + 1, cols]
                v_buf[_block(k), cols] = v
                vb_buf[_block(k), cols] = v.astype(_BF16)

    lam = lam_ref[...]
    softplus_neg_lam = jnp.maximum(-lam, 0.0) + jnp.log1p(jnp.exp(-jnp.abs(lam)))
    log_a_scale = -LRU_C * softplus_neg_lam

    def rnn_head(hd):
        cols = head_cols(hd)
        ri = _sigmoid_of_twice(_dot(vb_buf[:, cols], w_ax_ref[cols, :])
                               + SIGMOID_INPUT_SCALE * b_ax_ref[hd:hd + 1, :])
        r = ri[:, 0:LANE]
        i = ri[:, LANE:2 * LANE]
        log_a = r * log_a_scale[:, cols]
        a = jnp.exp(log_a)
        b = _sqrt_nonneg(-jnp.tanh(log_a) * (1.0 + a * a)) * i * v_buf[:, cols]
        a_buf[hd] = a
        b_buf[hd] = b

        a_cum = a_buf[hd, _block(0), :]
        b_cum = b_buf[hd, _block(0), :]
        for k in range(1, PHASES):
            a_k = a_buf[hd, _block(k), :]
            b_cum = a_k * b_cum + b_buf[hd, _block(k), :]
            a_cum = a_k * a_cum
        d = 1
        while d < G:
            a_sh = jnp.where(row_g < d, 1.0, pltpu.roll(a_cum, d, axis=0))
            b_sh = jnp.where(row_g < d, 0.0, pltpu.roll(b_cum, d, axis=0))
            b_cum = a_cum * b_sh + b_cum
            a_cum = a_cum * a_sh
            d *= 2
        h0 = h_carry[hd:hd + 1, :]
        h_out = a_cum * h0 + b_cum
        h_carry[hd:hd + 1, :] = h_out[G - 1:G, :]
        h_in = previous_group(h_out, h0)
        h_k = h_in
        for k in range(PHASES):
            h_k = a_buf[hd, _block(k), :] * h_k + b_buf[hd, _block(k), :]
            hg_buf[_block(k), cols] = (h_k * _gelu_tanh(gate_buf[_block(k), cols])).astype(_BF16)

    def step_body(mixers, ffn):
        assert len(FF_CHUNK_STARTS) == 6 and N_HEADS == 8

        def mixer_piece(fn, *args):
            return fn(*args) if mixers else None

        def ffn_piece(fn, *args):
            return fn(*args) if ffn else None

        def input_projections():
            h_buf[...] = _rmsnorm(x_buf[x_slot], g_mix_ref[...]).astype(_BF16)
            urnn_buf[...] = _dot(h_buf[...], w_in_ref[:, O_RNN:O_GATE])
            upool_buf[...] = _dot(h_buf[...], w_in_ref[:, 0:O_RNN])
            gate_buf[...] = _dot(h_buf[...], w_in_ref[:, O_GATE:O_GPOOL])

        def pool_gate():
            sg_pool_buf[...] = _sigmoid_of_twice(_dot(h_buf[...], w_in_ref[:, O_GPOOL:O_GRNN]))

        def rnn_gate():
            sg_rnn_buf[...] = _sigmoid_of_twice(
                _dot(h_buf[...], w_in_ref[:, O_GRNN:O_GRNN + D_MODEL]))

        def merge_and_residual(y_pool, y_rnn):
            mix = sg_pool_buf[...] * y_pool + sg_rnn_buf[...] * y_rnn
            x1 = x_buf[x_slot] + _dot(mix.astype(_BF16), w_o_ref[...])
            x1_buf[...] = x1
            h2_buf[...] = _rmsnorm(x1, g_ffn_ref[...]).astype(_BF16)

        ffn_piece(ffn_chunk, 0)
        mixer_piece(input_projections)
        mixer_piece(conv_all_heads)
        mixer_piece(pool_windows)
        for hd in range(5):
            mixer_piece(rnn_head, hd)
            ffn_piece(ffn_chunk, hd + 1)
        mixer_piece(rnn_head, 5)
        mixer_piece(pool_gate)
        mixer_piece(rnn_head, 6)
        mixer_piece(pool_group_dots)
        mixer_piece(rnn_gate)
        mixer_piece(rnn_head, 7)
        y_pool = mixer_piece(_dot, mixed_buf[...], w_pool_out_ref[...]) if mixers else None
        ffn_piece(ffn_out_first_half)
        y_rnn = mixer_piece(_dot, hg_buf[...], w_rnn_out_ref[...]) if mixers else None
        ffn_piece(ffn_out_second_half)
        mixer_piece(merge_and_residual, y_pool, y_rnn)

    @pl.when(j == 0)
    def _():
        step_body(mixers=True, ffn=False)

    @pl.when(jnp.logical_and(j > 0, j < n_tiles))
    def _():
        step_body(mixers=True, ffn=True)

    @pl.when(j == n_tiles)
    def _():
        step_body(mixers=False, ffn=True)

    @pl.when(j >= 1)
    def _():
        for_phases(y_copy, j - 1, "start")

    @pl.when(j == n_tiles)
    def _():
        for_phases(y_copy, n_tiles - 2, "wait")
        for_phases(y_copy, n_tiles - 1, "wait")


def _resident(shape):
    n = len(shape)
    return pl.BlockSpec(shape, lambda j: (0,) * n, pipeline_mode=pl.Buffered(1))


@jax.jit
def kernel(x, norm_mix, w_in, w_pool_grp, pool_scale, w_pool_out, conv_w, conv_b, w_rg_a, b_rg_a,
           w_rg_x, b_rg_x, lru_lambda, w_rnn_out, w_o, norm_ffn, w_ffn_in, w_ffn_out, norm_final):
    B, S, D = x.shape
    assert D == D_MODEL and S % SEQ_TILE == 0
    assert norm_mix.shape[0] == 1, "single-layer stack"
    T = SEQ_TILE
    nt = S // T
    n_tiles = B * nt
    assert n_tiles >= N_SLOTS

    b_ax = jnp.concatenate([b_rg_a[0], b_rg_x[0]], axis=-1)
    in_hbm = pl.BlockSpec(memory_space=pl.ANY)
    operands_and_specs = [
        (x.reshape(B, S // PHASES, PHASES, D), in_hbm),
        (norm_mix[0].reshape(1, D), None),
        (w_in[0], in_hbm),
        (w_pool_grp[0].reshape(D_POOL, LANE), in_hbm),
        (pool_scale[0].reshape(1, D_POOL), None),
        (w_pool_out[0], in_hbm),
        (conv_w[0], None),
        (conv_b[0].reshape(1, D_RNN), None),
        (w_rg_a[0].reshape(D_RNN, LANE), in_hbm),
        (w_rg_x[0].reshape(D_RNN, LANE), in_hbm),
        (b_ax, None),
        (lru_lambda[0].reshape(1, D_RNN), None),
        (w_rnn_out[0], in_hbm),
        (w_o[0], in_hbm),
        (norm_ffn[0].reshape(1, D), None),
        (w_ffn_in[0], in_hbm),
        (w_ffn_out[0], in_hbm),
        (norm_final.reshape(1, D), None),
    ]
    operands = [op for op, _ in operands_and_specs]
    in_specs = [spec if spec is not None else _resident(op.shape)
                for op, spec in operands_and_specs]
    n_carry_rows = len(POOL_SHIFTS) * PHASES
    d_in = w_in.shape[-1]
    assert d_in == O_GRNN + D_MODEL and d_in <= 2 * D_FF

    out = pl.pallas_call(
        functools.partial(_layer_kernel, nt, n_tiles),
        grid=(n_tiles + 1,),
        in_specs=in_specs,
        out_specs=pl.BlockSpec(memory_space=pl.ANY),
        out_shape=jax.ShapeDtypeStruct((B, S // PHASES, PHASES, D), x.dtype),
        scratch_shapes=[
            pltpu.VMEM((N_SLOTS, T, D), _F32),
            pltpu.VMEM((N_SLOTS, T, D), _F32),
            pltpu.SemaphoreType.DMA((N_SLOTS,)),
            pltpu.SemaphoreType.DMA((N_SLOTS,)),
            pltpu.VMEM((D, d_in), _BF16),
            pltpu.VMEM((D_POOL, LANE), _BF16),
            pltpu.VMEM((D_POOL, D), _BF16),
            pltpu.VMEM((D_RNN, 2 * LANE), _BF16),
            pltpu.VMEM((D_RNN, D), _BF16),
            pltpu.VMEM((D, D), _BF16),
            pltpu.VMEM((D, 2 * D_FF), _BF16),
            pltpu.VMEM((D_FF, D), _BF16),
            pltpu.VMEM((WIDE_STAGE_SLOTS, WIDE_STAGE_ROWS, 2 * D_FF), _F32),
            pltpu.VMEM((NARROW_STAGE_SLOTS, NARROW_STAGE_ROWS, D), _F32),
            pltpu.SemaphoreType.DMA((WIDE_STAGE_SLOTS,)),
            pltpu.SemaphoreType.DMA((NARROW_STAGE_SLOTS,)),
            pltpu.VMEM((T, D_POOL), _F32),
            pltpu.VMEM((T, D_RNN), _F32),
            pltpu.VMEM((T, D_RNN), _F32),
            pltpu.VMEM((T, D_RNN), _F32),
            pltpu.VMEM((T, D_RNN), _BF16),
            pltpu.VMEM((N_HEADS, T, LANE), _F32),
            pltpu.VMEM((N_HEADS, T, LANE), _F32),
            pltpu.VMEM((T, D_RNN), _BF16),
            pltpu.VMEM((T, D_FF), _BF16),
            pltpu.VMEM((T, D), _F32),
            pltpu.VMEM((T, D), _BF16),
            pltpu.VMEM((T, D), _BF16),
            pltpu.VMEM((T, D_POOL), _BF16),
            pltpu.VMEM((T, D_POOL), _BF16),
            pltpu.VMEM((T, D), _F32),
            pltpu.VMEM((T, D), _F32),
            pltpu.VMEM((T, D), _F32),
            pltpu.VMEM((n_carry_rows, D_POOL), _F32),
            pltpu.VMEM((PHASES, D_RNN), _F32),
            pltpu.VMEM((N_HEADS, LANE), _F32),
        ],
        compiler_params=pltpu.CompilerParams(
            dimension_semantics=("arbitrary",),
            vmem_limit_bytes=VMEM_LIMIT_BYTES,
        ),
        name="hybrid_layer",
    )(*operands)
    return out.reshape(B, S, D)
```

```python
import functools
import math

import jax
import jax.numpy as jnp
from jax.experimental import pallas as pl
from jax.experimental.pallas import tpu as pltpu

D_MODEL = 1024
POOL_WINDOWS = (2, 4, 8, 16)
LANE = 128
PHASES = 8
D_POOL = LANE * len(POOL_WINDOWS)
D_RNN = D_MODEL
N_HEADS = D_RNN // LANE
CONV_WIDTH = 4
LRU_C = 8.0
D_FF = 2816
NORM_EPS = 1e-6

SEQ_TILE = 256
GROUPS = SEQ_TILE // PHASES
N_SLOTS = 2
FF_CHUNK = 512
FF_CHUNK_STARTS = tuple(range(0, D_FF, FF_CHUNK))
FF_OUT_SPLIT = 1536
POOL_SHIFTS = (1, 2, 4, 8)
WIDE_STAGE_ROWS = 32
WIDE_STAGE_SLOTS = 6
NARROW_STAGE_ROWS = 256
NARROW_STAGE_SLOTS = 4
VMEM_LIMIT_BYTES = 60 * 1024 * 1024

O_RNN = D_POOL
O_GATE = O_RNN + D_RNN
O_GPOOL = O_GATE + D_RNN
O_GRNN = O_GPOOL + D_MODEL

_F32 = jnp.float32
_BF16 = jnp.bfloat16


def _dot(a, b):
    return jnp.dot(a, b, preferred_element_type=_F32)


def _rmsnorm(x, g):
    return x * jax.lax.rsqrt(jnp.mean(x * x, axis=-1, keepdims=True) + NORM_EPS) * g


def _sigmoid_of_twice(z):
    return 0.5 * jnp.tanh(z) + 0.5


def _gelu_tanh(x):
    c = math.sqrt(2.0 / math.pi)
    return 0.5 * x * (1.0 + jnp.tanh(c * (x + 0.044715 * (x * x * x))))


def _sqrt_nonneg(x):
    return jnp.where(x > 0.0, x * jax.lax.rsqrt(x), 0.0)


def _block(k):
    return slice(k * GROUPS, (k + 1) * GROUPS)


def _load_weights_as_bf16(jobs, stage, sem):
    n_slots = stage.shape[0]

    def copy(i):
        src, r0, rows, _, _ = jobs[i]
        return pltpu.make_async_copy(src.at[pl.ds(r0, rows), :],
                                     stage.at[i % n_slots, pl.ds(0, rows), pl.ds(0, src.shape[1])],
                                     sem.at[i % n_slots])

    for i in range(min(n_slots, len(jobs))):
        copy(i).start()
    for i, (_, r0, rows, dst, col_map) in enumerate(jobs):
        copy(i).wait()
        for src_c0, dst_c0, width, scale in col_map:
            w = stage[i % n_slots, 0:rows, src_c0:src_c0 + width]
            if scale != 1.0:
                assert math.frexp(scale)[0] == 0.5, "only power-of-two scales are exact"
                w = w * scale
            dst[r0:r0 + rows, dst_c0:dst_c0 + width] = w.astype(_BF16)
        if i + n_slots < len(jobs):
            copy(i + n_slots).start()


def _row_chunks(src, dst, rows_per_chunk, col_map=None, dst_first_col=0, scale=1.0):
    n_rows, n_cols = src.shape
    assert n_rows % rows_per_chunk == 0
    if col_map is None:
        col_map = [(0, dst_first_col, n_cols, scale)]
    return [(src, r0, rows_per_chunk, dst, col_map) for r0 in range(0, n_rows, rows_per_chunk)]


SIGMOID_INPUT_SCALE = 0.5
FFN_IN_COL_MAP = (
    [(n * LANE, 2 * n * LANE, LANE, SIGMOID_INPUT_SCALE) for n in range(D_FF // LANE)]
    + [(D_FF + n * LANE, (2 * n + 1) * LANE, LANE, 1.0) for n in range(D_FF // LANE)])
W_IN_COL_MAP = [(0, 0, O_GPOOL, 1.0), (O_GPOOL, O_GPOOL, 2 * D_MODEL, SIGMOID_INPUT_SCALE)]


def _layer_kernel(tiles_per_seq, n_tiles,
                  x_hbm, g_mix_ref, w_in_hbm, w_grp_hbm, pool_scale_ref, w_pool_out_hbm,
                  conv_w_ref, conv_b_ref, w_rg_a_hbm, w_rg_x_hbm, b_ax_ref, lam_ref,
                  w_rnn_out_hbm, w_o_hbm, g_ffn_ref, w_ffn_in_hbm, w_ffn_out_hbm, g_final_ref,
                  o_hbm,
                  x_buf, y_buf, in_sem, out_sem,
                  w_in_ref, w_pool_eff_ref, w_ax_ref, w_rnn_out_ref, w_o_ref,
                  w_ffn_in_ref, w_ffn_out_ref, stage_wide, stage_narrow, wide_sem, narrow_sem,
                  upool_buf, urnn_buf, gate_buf, v_buf, vb_buf, a_buf, b_buf, hg_buf, act_buf,
                  x1_buf, h_buf, h2_buf, pooled_buf, sg_pool_buf, sg_rnn_buf,
                  ffn_acc, pool_carry, conv_carry, h_carry):
    T = SEQ_TILE
    G = GROUPS
    j = pl.program_id(0)
    s = j % tiles_per_seq
    x_slot = j % N_SLOTS
    y_slot = (j + 1) % N_SLOTS

    def head_cols(hd):
        return slice(hd * LANE, (hd + 1) * LANE)

    def tile_phase_hbm(ref, t, k):
        return ref.at[t // tiles_per_seq, pl.ds((t % tiles_per_seq) * G, G), k, :]

    def x_copy(t, k):
        return pltpu.make_async_copy(tile_phase_hbm(x_hbm, t, k),
                                     x_buf.at[t % N_SLOTS, pl.ds(k * G, G), :],
                                     in_sem.at[t % N_SLOTS])

    def y_copy(t, k):
        return pltpu.make_async_copy(y_buf.at[t % N_SLOTS, pl.ds(k * G, G), :],
                                     tile_phase_hbm(o_hbm, t, k),
                                     out_sem.at[t % N_SLOTS])

    def for_phases(copy, t, action):
        for k in range(PHASES):
            getattr(copy(t, k), action)()

    @pl.when(j == 0)
    def _():
        for_phases(x_copy, 0, "start")
        x1_buf[...] = jnp.zeros_like(x1_buf)
        h2_buf[...] = jnp.zeros_like(h2_buf)
        _load_weights_as_bf16(
            _row_chunks(w_in_hbm, w_in_ref, WIDE_STAGE_ROWS, col_map=W_IN_COL_MAP)
            + _row_chunks(w_ffn_in_hbm, w_ffn_in_ref, WIDE_STAGE_ROWS, col_map=FFN_IN_COL_MAP),
            stage_wide, wide_sem)
        _load_weights_as_bf16(
            _row_chunks(w_ffn_out_hbm, w_ffn_out_ref, NARROW_STAGE_ROWS)
            + _row_chunks(w_rnn_out_hbm, w_rnn_out_ref, NARROW_STAGE_ROWS)
            + _row_chunks(w_o_hbm, w_o_ref, NARROW_STAGE_ROWS)
            + _row_chunks(w_rg_a_hbm, w_ax_ref, NARROW_STAGE_ROWS, scale=SIGMOID_INPUT_SCALE)
            + _row_chunks(w_rg_x_hbm, w_ax_ref, NARROW_STAGE_ROWS, dst_first_col=LANE,
                          scale=SIGMOID_INPUT_SCALE),
            stage_narrow, narrow_sem)
        assert NARROW_STAGE_SLOTS >= 4 and D_POOL == 2 * NARROW_STAGE_ROWS
        pool_copies = []
        for half in range(2):
            rows = pl.ds(half * NARROW_STAGE_ROWS, NARROW_STAGE_ROWS)
            pool_copies.append(pltpu.make_async_copy(
                w_pool_out_hbm.at[rows, :], stage_narrow.at[half], narrow_sem.at[half]))
            pool_copies.append(pltpu.make_async_copy(
                w_grp_hbm.at[rows, :], stage_narrow.at[2 + half, :, pl.ds(0, LANE)],
                narrow_sem.at[2 + half]))
        for c in pool_copies:
            c.start()
        for c in pool_copies:
            c.wait()
        for g in range(len(POOL_WINDOWS)):
            half, rows = divmod(g * LANE, NARROW_STAGE_ROWS)
            w_g = stage_narrow[2 + half, rows:rows + LANE, 0:LANE] * pool_scale_ref[:, head_cols(g)]
            w_pool_eff_ref[head_cols(g), :] = jnp.dot(
                w_g, stage_narrow[half, rows:rows + LANE, :],
                preferred_element_type=_F32, precision=jax.lax.Precision.HIGHEST).astype(_BF16)

    @pl.when(j + 1 < n_tiles)
    def _():
        for_phases(x_copy, j + 1, "start")

    @pl.when(j < n_tiles)
    def _():
        for_phases(x_copy, j, "wait")

    @pl.when(j >= N_SLOTS + 1)
    def _():
        for_phases(y_copy, j - 1 - N_SLOTS, "wait")

    @pl.when(s == 0)
    def _():
        pool_carry[...] = jnp.zeros_like(pool_carry)
        conv_carry[...] = jnp.zeros_like(conv_carry)
        h_carry[...] = jnp.zeros_like(h_carry)

    row_g = jax.lax.broadcasted_iota(jnp.int32, (G, LANE), 0)

    def previous_group(blk, last_row_before_tile):
        return jnp.where(row_g == 0, last_row_before_tile, pltpu.roll(blk, 1, axis=0))

    def ffn_chunk(n):
        c = FF_CHUNK_STARTS[n]
        wd = min(FF_CHUNK, D_FF - c)
        gate_up = _dot(h2_buf[...], w_ffn_in_ref[:, 2 * c:2 * (c + wd)])
        for blk in range(wd // LANE):
            half_gate = gate_up[:, 2 * blk * LANE:(2 * blk + 1) * LANE]
            up = gate_up[:, (2 * blk + 1) * LANE:(2 * blk + 2) * LANE]
            act_buf[:, c + blk * LANE:c + (blk + 1) * LANE] = (
                half_gate * (jnp.tanh(half_gate) + 1.0) * up).astype(_BF16)

    def ffn_out_first_half():
        ffn_acc[...] = x1_buf[...] + _dot(act_buf[:, 0:FF_OUT_SPLIT],
                                          w_ffn_out_ref[0:FF_OUT_SPLIT, :])

    def ffn_out_second_half():
        x2 = ffn_acc[...] + _dot(act_buf[:, FF_OUT_SPLIT:D_FF], w_ffn_out_ref[FF_OUT_SPLIT:D_FF, :])
        y_buf[y_slot] = _rmsnorm(x2, g_final_ref[...])

    def pool_windows():
        first_groups = jax.lax.broadcasted_iota(jnp.int32, (PHASES, LANE), 0)
        for g, w in enumerate(POOL_WINDOWS):
            cols = head_cols(g)
            tok = [upool_buf[_block(k), cols] for k in range(PHASES)]
            level = tok
            for i, d in enumerate(POOL_SHIFTS):
                if d >= w:
                    break
                shifted = []
                for k in range(PHASES):
                    if k >= d:
                        shifted.append(level[k - d])
                    else:
                        src = k - d + PHASES
                        row = i * PHASES + src
                        shifted.append(previous_group(level[src], pool_carry[row:row + 1, cols]))
                for src in range(PHASES - d, PHASES):
                    row = i * PHASES + src
                    pool_carry[row:row + 1, cols] = level[src][G - 1:G, :]
                level = [level[k] + shifted[k] for k in range(PHASES)]
            for k in range(PHASES):
                pos1 = s * T + PHASES * first_groups + (k + 1)
                inv_head = 1.0 / jnp.minimum(pos1, w).astype(_F32)
                inv_count = jnp.concatenate(
                    [inv_head, jnp.full((G - PHASES, LANE), 1.0 / w, _F32)], axis=0)
                pooled = level[k] * inv_count - tok[k]
                pooled_buf[_block(k), cols] = pooled.astype(_BF16)

    def conv_all_heads():
        for hd in range(N_HEADS):
            cols = head_cols(hd)
            tok = [urnn_buf[_block(k), cols] for k in range(PHASES)]
            prev = {}
            for src in range(PHASES - CONV_WIDTH + 1, PHASES):
                prev[src] = previous_group(tok[src], conv_carry[src:src + 1, cols])
                conv_carry[src:src + 1, cols] = tok[src][G - 1:G, :]
            for k in range(PHASES):
                v = conv_b_ref[:, cols]
                for tap in range(CONV_WIDTH):
                    m = CONV_WIDTH - 1 - tap
                    u = tok[k - m] if k >= m else prev[k - m + PHASES]
                    v = v + u * conv_w_ref[tap:tap + 1, cols]
                v_buf[_block(k), cols] = v
                vb_buf[_block(k), cols] = v.astype(_BF16)

    lam = lam_ref[...]
    softplus_neg_lam = jnp.maximum(-lam, 0.0) + jnp.log1p(jnp.exp(-jnp.abs(lam)))
    log_a_scale = -LRU_C * softplus_neg_lam

    def rnn_head(hd):
        cols = head_cols(hd)
        ri = _sigmoid_of_twice(_dot(vb_buf[:, cols], w_ax_ref[cols, :])
                               + SIGMOID_INPUT_SCALE * b_ax_ref[hd:hd + 1, :])
        r = ri[:, 0:LANE]
        i = ri[:, LANE:2 * LANE]
        log_a = r * log_a_scale[:, cols]
        a = jnp.exp(log_a)
        b = _sqrt_nonneg(-jnp.tanh(log_a) * (1.0 + a * a)) * i * v_buf[:, cols]
        a_buf[hd] = a
        b_buf[hd] = b

        a_cum = a_buf[hd, _block(0), :]
        b_cum = b_buf[hd, _block(0), :]
        for k in range(1, PHASES):
            a_k = a_buf[hd, _block(k), :]
            b_cum = a_k * b_cum + b_buf[hd, _block(k), :]
            a_cum = a_k * a_cum
        d = 1
        while d < G:
            a_sh = jnp.where(row_g < d, 1.0, pltpu.roll(a_cum, d, axis=0))
            b_sh = jnp.where(row_g < d, 0.0, pltpu.roll(b_cum, d, axis=0))
            b_cum = a_cum * b_sh + b_cum
            a_cum = a_cum * a_sh
            d *= 2
        h0 = h_carry[hd:hd + 1, :]
        h_out = a_cum * h0 + b_cum
        h_carry[hd:hd + 1, :] = h_out[G - 1:G, :]
        h_in = previous_group(h_out, h0)
        h_k = h_in
        for k in range(PHASES):
            h_k = a_buf[hd, _block(k), :] * h_k + b_buf[hd, _block(k), :]
            hg_buf[_block(k), cols] = (h_k * _gelu_tanh(gate_buf[_block(k), cols])).astype(_BF16)

    def step_body(mixers, ffn):
        assert len(FF_CHUNK_STARTS) == 6 and N_HEADS == 8

        def mixer_piece(fn, *args):
            return fn(*args) if mixers else None

        def ffn_piece(fn, *args):
            return fn(*args) if ffn else None

        def input_projections():
            h_buf[...] = _rmsnorm(x_buf[x_slot], g_mix_ref[...]).astype(_BF16)
            urnn_buf[...] = _dot(h_buf[...], w_in_ref[:, O_RNN:O_GATE])
            upool_buf[...] = _dot(h_buf[...], w_in_ref[:, 0:O_RNN])
            gate_buf[...] = _dot(h_buf[...], w_in_ref[:, O_GATE:O_GPOOL])

        def gated_pool_branch():
            y_pool = _dot(pooled_buf[...], w_pool_eff_ref[...])
            sg_pool_buf[...] = y_pool * _sigmoid_of_twice(
                _dot(h_buf[...], w_in_ref[:, O_GPOOL:O_GRNN]))

        def rnn_gate():
            sg_rnn_buf[...] = _sigmoid_of_twice(
                _dot(h_buf[...], w_in_ref[:, O_GRNN:O_GRNN + D_MODEL]))

        def merge_and_residual(y_rnn):
            mix = sg_pool_buf[...] + sg_rnn_buf[...] * y_rnn
            x1 = x_buf[x_slot] + _dot(mix.astype(_BF16), w_o_ref[...])
            x1_buf[...] = x1
            h2_buf[...] = _rmsnorm(x1, g_ffn_ref[...]).astype(_BF16)

        ffn_piece(ffn_chunk, 0)
        mixer_piece(input_projections)
        mixer_piece(conv_all_heads)
        mixer_piece(pool_windows)
        for hd in range(5):
            mixer_piece(rnn_head, hd)
            ffn_piece(ffn_chunk, hd + 1)
        mixer_piece(rnn_head, 5)
        mixer_piece(gated_pool_branch)
        mixer_piece(rnn_head, 6)
        mixer_piece(rnn_gate)
        mixer_piece(rnn_head, 7)
        ffn_piece(ffn_out_first_half)
        y_rnn = mixer_piece(_dot, hg_buf[...], w_rnn_out_ref[...]) if mixers else None
        ffn_piece(ffn_out_second_half)
        mixer_piece(merge_and_residual, y_rnn)

    step_body(mixers=True, ffn=True)

    @pl.when(j >= 1)
    def _():
        for_phases(y_copy, j - 1, "start")

    @pl.when(j == n_tiles)
    def _():
        for_phases(y_copy, n_tiles - 2, "wait")
        for_phases(y_copy, n_tiles - 1, "wait")


def _resident(shape):
    n = len(shape)
    return pl.BlockSpec(shape, lambda j: (0,) * n, pipeline_mode=pl.Buffered(1))


@jax.jit
def kernel(x, norm_mix, w_in, w_pool_grp, pool_scale, w_pool_out, conv_w, conv_b, w_rg_a, b_rg_a,
           w_rg_x, b_rg_x, lru_lambda, w_rnn_out, w_o, norm_ffn, w_ffn_in, w_ffn_out, norm_final):
    B, S, D = x.shape
    assert D == D_MODEL and S % SEQ_TILE == 0
    assert norm_mix.shape[0] == 1, "single-layer stack"
    T = SEQ_TILE
    nt = S // T
    n_tiles = B * nt
    assert n_tiles >= N_SLOTS

    b_ax = jnp.concatenate([b_rg_a[0], b_rg_x[0]], axis=-1)
    in_hbm = pl.BlockSpec(memory_space=pl.ANY)
    operands_and_specs = [
        (x.reshape(B, S // PHASES, PHASES, D), in_hbm),
        (norm_mix[0].reshape(1, D), None),
        (w_in[0], in_hbm),
        (w_pool_grp[0].reshape(D_POOL, LANE), in_hbm),
        (pool_scale[0].reshape(1, D_POOL), None),
        (w_pool_out[0], in_hbm),
        (conv_w[0], None),
        (conv_b[0].reshape(1, D_RNN), None),
        (w_rg_a[0].reshape(D_RNN, LANE), in_hbm),
        (w_rg_x[0].reshape(D_RNN, LANE), in_hbm),
        (b_ax, None),
        (lru_lambda[0].reshape(1, D_RNN), None),
        (w_rnn_out[0], in_hbm),
        (w_o[0], in_hbm),
        (norm_ffn[0].reshape(1, D), None),
        (w_ffn_in[0], in_hbm),
        (w_ffn_out[0], in_hbm),
        (norm_final.reshape(1, D), None),
    ]
    operands = [op for op, _ in operands_and_specs]
    in_specs = [spec if spec is not None else _resident(op.shape)
                for op, spec in operands_and_specs]
    n_carry_rows = len(POOL_SHIFTS) * PHASES
    d_in = w_in.shape[-1]
    assert d_in == O_GRNN + D_MODEL and d_in <= 2 * D_FF

    out = pl.pallas_call(
        functools.partial(_layer_kernel, nt, n_tiles),
        grid=(n_tiles + 1,),
        in_specs=in_specs,
        out_specs=pl.BlockSpec(memory_space=pl.ANY),
        out_shape=jax.ShapeDtypeStruct((B, S // PHASES, PHASES, D), x.dtype),
        scratch_shapes=[
            pltpu.VMEM((N_SLOTS, T, D), _F32),
            pltpu.VMEM((N_SLOTS, T, D), _F32),
            pltpu.SemaphoreType.DMA((N_SLOTS,)),
            pltpu.SemaphoreType.DMA((N_SLOTS,)),
            pltpu.VMEM((D, d_in), _BF16),
            pltpu.VMEM((D_POOL, D), _BF16),
            pltpu.VMEM((D_RNN, 2 * LANE), _BF16),
            pltpu.VMEM((D_RNN, D), _BF16),
            pltpu.VMEM((D, D), _BF16),
            pltpu.VMEM((D, 2 * D_FF), _BF16),
            pltpu.VMEM((D_FF, D), _BF16),
            pltpu.VMEM((WIDE_STAGE_SLOTS, WIDE_STAGE_ROWS, 2 * D_FF), _F32),
            pltpu.VMEM((NARROW_STAGE_SLOTS, NARROW_STAGE_ROWS, D), _F32),
            pltpu.SemaphoreType.DMA((WIDE_STAGE_SLOTS,)),
            pltpu.SemaphoreType.DMA((NARROW_STAGE_SLOTS,)),
            pltpu.VMEM((T, D_POOL), _F32),
            pltpu.VMEM((T, D_RNN), _F32),
            pltpu.VMEM((T, D_RNN), _F32),
            pltpu.VMEM((T, D_RNN), _F32),
            pltpu.VMEM((T, D_RNN), _BF16),
            pltpu.VMEM((N_HEADS, T, LANE), _F32),
            pltpu.VMEM((N_HEADS, T, LANE), _F32),
            pltpu.VMEM((T, D_RNN), _BF16),
            pltpu.VMEM((T, D_FF), _BF16),
            pltpu.VMEM((T, D), _F32),
            pltpu.VMEM((T, D), _BF16),
            pltpu.VMEM((T, D), _BF16),
            pltpu.VMEM((T, D_POOL), _BF16),
            pltpu.VMEM((T, D), _F32),
            pltpu.VMEM((T, D), _F32),
            pltpu.VMEM((T, D), _F32),
            pltpu.VMEM((n_carry_rows, D_POOL), _F32),
            pltpu.VMEM((PHASES, D_RNN), _F32),
            pltpu.VMEM((N_HEADS, LANE), _F32),
        ],
        compiler_params=pltpu.CompilerParams(
            dimension_semantics=("arbitrary",),
            vmem_limit_bytes=VMEM_LIMIT_BYTES,
        ),
        name="hybrid_layer",
    )(*operands)
    return out.reshape(B, S, D)
```

```python
import functools
import math

import jax
import jax.numpy as jnp
from jax.experimental import pallas as pl
from jax.experimental.pallas import tpu as pltpu

D_MODEL = 1024
POOL_WINDOWS = (2, 4, 8, 16)
LANE = 128
PHASES = 8
D_POOL = LANE * len(POOL_WINDOWS)
D_RNN = D_MODEL
N_HEADS = D_RNN // LANE
CONV_WIDTH = 4
LRU_C = 8.0
D_FF = 2816
NORM_EPS = 1e-6

SEQ_TILE = 256
GROUPS = SEQ_TILE // PHASES
N_SLOTS = 2
FF_CHUNK = 512
FF_CHUNK_STARTS = tuple(range(0, D_FF, FF_CHUNK))
FF_OUT_SPLIT = 1536
POOL_SHIFTS = (1, 2, 4, 8)
WIDE_STAGE_ROWS = 32
WIDE_STAGE_SLOTS = 8
NARROW_STAGE_ROWS = 256
NARROW_STAGE_SLOTS = 6
VMEM_LIMIT_BYTES = 60 * 1024 * 1024

O_RNN = D_POOL
O_GATE = O_RNN + D_RNN
O_GPOOL = O_GATE + D_RNN
O_GRNN = O_GPOOL + D_MODEL

_F32 = jnp.float32
_BF16 = jnp.bfloat16


def _dot(a, b):
    return jnp.dot(a, b, preferred_element_type=_F32)


def _rmsnorm(x, g):
    return x * jax.lax.rsqrt(jnp.mean(x * x, axis=-1, keepdims=True) + NORM_EPS) * g


def _sigmoid_of_twice(z):
    return 0.5 * jnp.tanh(z) + 0.5


def _gelu_tanh_of_twice(g):
    c = math.sqrt(2.0 / math.pi)
    return g * (1.0 + jnp.tanh(g * (2.0 * c + (8.0 * c * 0.044715) * (g * g))))


def _sqrt_nonneg(x):
    return jnp.where(x > 0.0, x * jax.lax.rsqrt(x), 0.0)


def _block(k):
    return slice(k * GROUPS, (k + 1) * GROUPS)


class _WeightRing:
    def __init__(self, jobs, stage, sem):
        self.jobs, self.stage, self.sem = jobs, stage, sem
        self.n_slots = stage.shape[0]

    def _copy(self, i):
        src, r0, rows, _, _ = self.jobs[i]
        slot = i % self.n_slots
        return pltpu.make_async_copy(
            src.at[pl.ds(r0, rows), :],
            self.stage.at[slot, pl.ds(0, rows), pl.ds(0, src.shape[1])], self.sem.at[slot])

    def prime(self):
        for i in range(min(self.n_slots, len(self.jobs))):
            self._copy(i).start()

    def convert_all(self):
        for i, (_, r0, rows, dst, col_map) in enumerate(self.jobs):
            self._copy(i).wait()
            for src_c0, dst_c0, width, scale in col_map:
                w = self.stage[i % self.n_slots, 0:rows, src_c0:src_c0 + width]
                if scale != 1.0:
                    assert math.frexp(scale)[0] == 0.5, "only power-of-two scales are exact"
                    w = w * scale
                dst[r0:r0 + rows, dst_c0:dst_c0 + width] = w.astype(_BF16)
            if i + self.n_slots < len(self.jobs):
                self._copy(i + self.n_slots).start()


def _row_chunks(src, dst, rows_per_chunk, col_map=None, dst_first_col=0, scale=1.0):
    n_rows, n_cols = src.shape
    assert n_rows % rows_per_chunk == 0
    if col_map is None:
        col_map = [(0, dst_first_col, n_cols, scale)]
    return [(src, r0, rows_per_chunk, dst, col_map) for r0 in range(0, n_rows, rows_per_chunk)]


SIGMOID_INPUT_SCALE = 0.5
FFN_IN_COL_MAP = (
    [(n * LANE, 2 * n * LANE, LANE, SIGMOID_INPUT_SCALE) for n in range(D_FF // LANE)]
    + [(D_FF + n * LANE, (2 * n + 1) * LANE, LANE, 1.0) for n in range(D_FF // LANE)])
W_IN_COL_MAP = [(0, 0, O_GATE, 1.0),
                (O_GATE, O_GATE, D_RNN + 2 * D_MODEL, SIGMOID_INPUT_SCALE)]


def _layer_kernel(tiles_per_seq, n_tiles,
                  x_hbm, g_mix_ref, w_in_hbm, w_grp_hbm, pool_scale_ref, w_pool_out_hbm,
                  conv_w_ref, conv_b_ref, w_rg_a_hbm, w_rg_x_hbm, b_ax_ref, lam_ref,
                  w_rnn_out_hbm, w_o_hbm, g_ffn_ref, w_ffn_in_hbm, w_ffn_out_hbm, g_final_ref,
                  o_hbm,
                  x_buf, y_buf, in_sem, out_sem,
                  w_in_ref, w_grp_ref, w_pool_out_ref, w_ax_ref, w_rnn_out_ref, w_o_ref,
                  w_ffn_in_ref, w_ffn_out_ref, stage_wide, stage_narrow, wide_sem, narrow_sem,
                  upool_buf, urnn_buf, gate_buf, v_buf, vb_buf, a_buf, b_buf, hg_buf, act_buf,
                  x1_buf, h_buf, h2_buf, pooled_buf, mixed_buf, sg_pool_buf, sg_rnn_buf,
                  ffn_acc, pool_carry, conv_carry, h_carry):
    T = SEQ_TILE
    G = GROUPS
    j = pl.program_id(0)
    s = j % tiles_per_seq
    x_slot = j % N_SLOTS
    y_slot = (j + 1) % N_SLOTS

    def tile_phase_hbm(ref, t, k):
        return ref.at[t // tiles_per_seq, pl.ds((t % tiles_per_seq) * G, G), k, :]

    def x_copy(t, k):
        return pltpu.make_async_copy(tile_phase_hbm(x_hbm, t, k),
                                     x_buf.at[t % N_SLOTS, pl.ds(k * G, G), :],
                                     in_sem.at[t % N_SLOTS])

    def y_copy(t, k):
        return pltpu.make_async_copy(y_buf.at[t % N_SLOTS, pl.ds(k * G, G), :],
                                     tile_phase_hbm(o_hbm, t, k),
                                     out_sem.at[t % N_SLOTS])

    def for_phases(copy, t, action):
        for k in range(PHASES):
            getattr(copy(t, k), action)()

    @pl.when(j == 0)
    def _():
        for_phases(x_copy, 0, "start")
        x1_buf[...] = jnp.zeros_like(x1_buf)
        h2_buf[...] = jnp.zeros_like(h2_buf)
        wide_ring = _WeightRing(
            _row_chunks(w_in_hbm, w_in_ref, WIDE_STAGE_ROWS, col_map=W_IN_COL_MAP)
            + _row_chunks(w_ffn_in_hbm, w_ffn_in_ref, WIDE_STAGE_ROWS, col_map=FFN_IN_COL_MAP),
            stage_wide, wide_sem)
        narrow_ring = _WeightRing(
            _row_chunks(w_ffn_out_hbm, w_ffn_out_ref, NARROW_STAGE_ROWS)
            + _row_chunks(w_rnn_out_hbm, w_rnn_out_ref, NARROW_STAGE_ROWS)
            + _row_chunks(w_o_hbm, w_o_ref, NARROW_STAGE_ROWS)
            + _row_chunks(w_pool_out_hbm, w_pool_out_ref, NARROW_STAGE_ROWS)
            + _row_chunks(w_rg_a_hbm, w_ax_ref, NARROW_STAGE_ROWS)
            + _row_chunks(w_rg_x_hbm, w_ax_ref, NARROW_STAGE_ROWS, dst_first_col=LANE)
            + _row_chunks(w_grp_hbm, w_grp_ref, NARROW_STAGE_ROWS),
            stage_narrow, narrow_sem)
        wide_ring.prime()
        narrow_ring.prime()
        wide_ring.convert_all()
        narrow_ring.convert_all()

    @pl.when(j + 1 < n_tiles)
    def _():
        for_phases(x_copy, j + 1, "start")

    @pl.when(j < n_tiles)
    def _():
        for_phases(x_copy, j, "wait")

    @pl.when(j >= N_SLOTS + 1)
    def _():
        for_phases(y_copy, j - 1 - N_SLOTS, "wait")

    @pl.when(s == 0)
    def _():
        pool_carry[...] = jnp.zeros_like(pool_carry)
        conv_carry[...] = jnp.zeros_like(conv_carry)
        h_carry[...] = jnp.zeros_like(h_carry)

    def head_cols(hd):
        return slice(hd * LANE, (hd + 1) * LANE)

    row_g = jax.lax.broadcasted_iota(jnp.int32, (G, LANE), 0)

    def previous_group(blk, last_row_before_tile):
        return jnp.where(row_g == 0, last_row_before_tile, pltpu.roll(blk, 1, axis=0))

    def ffn_chunk(n):
        c = FF_CHUNK_STARTS[n]
        wd = min(FF_CHUNK, D_FF - c)
        gate_up = _dot(h2_buf[...], w_ffn_in_ref[:, 2 * c:2 * (c + wd)])
        for blk in range(wd // LANE):
            half_gate = gate_up[:, 2 * blk * LANE:(2 * blk + 1) * LANE]
            up = gate_up[:, (2 * blk + 1) * LANE:(2 * blk + 2) * LANE]
            act_buf[:, c + blk * LANE:c + (blk + 1) * LANE] = (
                half_gate * (jnp.tanh(half_gate) + 1.0) * up).astype(_BF16)

    def ffn_out_first_half():
        ffn_acc[...] = x1_buf[...] + _dot(act_buf[:, 0:FF_OUT_SPLIT],
                                          w_ffn_out_ref[0:FF_OUT_SPLIT, :])

    def ffn_out_second_half():
        x2 = ffn_acc[...] + _dot(act_buf[:, FF_OUT_SPLIT:D_FF], w_ffn_out_ref[FF_OUT_SPLIT:D_FF, :])
        y_buf[y_slot] = _rmsnorm(x2, g_final_ref[...])

    def pool_windows():
        first_groups = jax.lax.broadcasted_iota(jnp.int32, (PHASES, LANE), 0)
        for g, w in enumerate(POOL_WINDOWS):
            cols = head_cols(g)
            tok = [upool_buf[_block(k), cols] for k in range(PHASES)]
            level = tok
            for i, d in enumerate(POOL_SHIFTS):
                if d >= w:
                    break
                shifted = []
                for k in range(PHASES):
                    if k >= d:
                        shifted.append(level[k - d])
                    else:
                        src = k - d + PHASES
                        row = i * PHASES + src
                        shifted.append(previous_group(level[src], pool_carry[row:row + 1, cols]))
                for src in range(PHASES - d, PHASES):
                    row = i * PHASES + src
                    pool_carry[row:row + 1, cols] = level[src][G - 1:G, :]
                level = [level[k] + shifted[k] for k in range(PHASES)]
            for k in range(PHASES):
                pos1 = s * T + PHASES * first_groups + (k + 1)
                inv_head = 1.0 / jnp.minimum(pos1, w).astype(_F32)
                inv_count = jnp.concatenate(
                    [inv_head, jnp.full((G - PHASES, LANE), 1.0 / w, _F32)], axis=0)
                pooled = level[k] * inv_count - tok[k]
                pooled_buf[_block(k), cols] = pooled.astype(_BF16)

    def pool_group_dots():
        for g in range(len(POOL_WINDOWS)):
            cols = head_cols(g)
            mixed = _dot(pooled_buf[:, cols], w_grp_ref[cols, :]) * pool_scale_ref[:, cols]
            mixed_buf[:, cols] = mixed.astype(_BF16)

    def conv_all_heads():
        for hd in range(N_HEADS):
            cols = head_cols(hd)
            tok = [urnn_buf[_block(k), cols] for k in range(PHASES)]
            prev = {}
            for src in range(PHASES - CONV_WIDTH + 1, PHASES):
                prev[src] = previous_group(tok[src], conv_carry[src:src + 1, cols])
                conv_carry[src:src + 1, cols] = tok[src][G - 1:G, :]
            half_conv_w = SIGMOID_INPUT_SCALE * conv_w_ref[:, cols]
            for k in range(PHASES):
                half_v = SIGMOID_INPUT_SCALE * conv_b_ref[:, cols]
                for tap in range(CONV_WIDTH):
                    m = CONV_WIDTH - 1 - tap
                    u = tok[k - m] if k >= m else prev[k - m + PHASES]
                    half_v = half_v + u * half_conv_w[tap:tap + 1, :]
                v_buf[_block(k), cols] = half_v
                vb_buf[_block(k), cols] = half_v.astype(_BF16)

    lam = lam_ref[...]
    softplus_neg_lam = jnp.maximum(-lam, 0.0) + jnp.log1p(jnp.exp(-jnp.abs(lam)))
    half_neg_log_a_scale = (SIGMOID_INPUT_SCALE * LRU_C) * softplus_neg_lam
    neg_log2_e = -1.0 / math.log(2.0)

    def rnn_head(hd):
        cols = head_cols(hd)
        tanh_ri = jnp.tanh(_dot(vb_buf[:, cols], w_ax_ref[cols, :])
                           + SIGMOID_INPUT_SCALE * b_ax_ref[hd:hd + 1, :])
        neg_log_a = (tanh_ri[:, 0:LANE] + 1.0) * half_neg_log_a_scale[:, cols]
        twice_i = tanh_ri[:, LANE:2 * LANE] + 1.0
        a = jnp.exp2(neg_log_a * neg_log2_e)
        b = _sqrt_nonneg(jnp.tanh(neg_log_a) * (1.0 + a * a)) * twice_i * v_buf[:, cols]
        a_buf[hd] = a
        b_buf[hd] = b

        a_cum = a_buf[hd, _block(0), :]
        b_cum = b_buf[hd, _block(0), :]
        for k in range(1, PHASES):
            a_k = a_buf[hd, _block(k), :]
            b_cum = a_k * b_cum + b_buf[hd, _block(k), :]
            a_cum = a_k * a_cum
        d = 1
        while d < G:
            a_sh = jnp.where(row_g < d, 1.0, pltpu.roll(a_cum, d, axis=0))
            b_sh = jnp.where(row_g < d, 0.0, pltpu.roll(b_cum, d, axis=0))
            b_cum = a_cum * b_sh + b_cum
            a_cum = a_cum * a_sh
            d *= 2
        h0 = h_carry[hd:hd + 1, :]
        h_out = a_cum * h0 + b_cum
        h_carry[hd:hd + 1, :] = h_out[G - 1:G, :]
        h_in = previous_group(h_out, h0)
        h_k = h_in
        for k in range(PHASES):
            h_k = a_buf[hd, _block(k), :] * h_k + b_buf[hd, _block(k), :]
            hg_buf[_block(k), cols] = (
                h_k * _gelu_tanh_of_twice(gate_buf[_block(k), cols])).astype(_BF16)

    assert len(FF_CHUNK_STARTS) == 6 and N_HEADS == 8
    ffn_chunk(0)
    h_buf[...] = _rmsnorm(x_buf[x_slot], g_mix_ref[...]).astype(_BF16)
    urnn_buf[...] = _dot(h_buf[...], w_in_ref[:, O_RNN:O_GATE])
    upool_buf[...] = _dot(h_buf[...], w_in_ref[:, 0:O_RNN])
    gate_buf[...] = _dot(h_buf[...], w_in_ref[:, O_GATE:O_GPOOL])
    conv_all_heads()
    pool_windows()
    rnn_head(0)
    ffn_chunk(1)
    rnn_head(1)
    ffn_chunk(2)
    rnn_head(2)
    ffn_chunk(3)
    rnn_head(3)
    ffn_chunk(4)
    rnn_head(4)
    ffn_chunk(5)
    rnn_head(5)
    sg_pool_buf[...] = _sigmoid_of_twice(_dot(h_buf[...], w_in_ref[:, O_GPOOL:O_GRNN]))
    rnn_head(6)
    pool_group_dots()
    sg_rnn_buf[...] = _sigmoid_of_twice(_dot(h_buf[...], w_in_ref[:, O_GRNN:O_GRNN + D_MODEL]))
    rnn_head(7)
    y_pool = _dot(mixed_buf[...], w_pool_out_ref[...])
    ffn_out_first_half()
    y_rnn = _dot(hg_buf[...], w_rnn_out_ref[...])
    ffn_out_second_half()
    mix = sg_pool_buf[...] * y_pool + sg_rnn_buf[...] * y_rnn
    x1 = x_buf[x_slot] + _dot(mix.astype(_BF16), w_o_ref[...])
    x1_buf[...] = x1
    h2_buf[...] = _rmsnorm(x1, g_ffn_ref[...]).astype(_BF16)

    @pl.when(j >= 1)
    def _():
        for_phases(y_copy, j - 1, "start")

    @pl.when(j == n_tiles)
    def _():
        for_phases(y_copy, n_tiles - 2, "wait")
        for_phases(y_copy, n_tiles - 1, "wait")


def _resident(shape):
    n = len(shape)
    return pl.BlockSpec(shape, lambda j: (0,) * n, pipeline_mode=pl.Buffered(1))


@jax.jit
def kernel(x, norm_mix, w_in, w_pool_grp, pool_scale, w_pool_out, conv_w, conv_b, w_rg_a, b_rg_a,
           w_rg_x, b_rg_x, lru_lambda, w_rnn_out, w_o, norm_ffn, w_ffn_in, w_ffn_out, norm_final):
    B, S, D = x.shape
    assert D == D_MODEL and S % SEQ_TILE == 0
    assert norm_mix.shape[0] == 1, "single-layer stack"
    T = SEQ_TILE
    nt = S // T
    n_tiles = B * nt
    assert n_tiles >= N_SLOTS

    b_ax = jnp.concatenate([b_rg_a[0], b_rg_x[0]], axis=-1)
    in_hbm = pl.BlockSpec(memory_space=pl.ANY)
    operands_and_specs = [
        (x.reshape(B, S // PHASES, PHASES, D), in_hbm),
        (norm_mix[0].reshape(1, D), None),
        (w_in[0], in_hbm),
        (w_pool_grp[0].reshape(D_POOL, LANE), in_hbm),
        (pool_scale[0].reshape(1, D_POOL), None),
        (w_pool_out[0], in_hbm),
        (conv_w[0], None),
        (conv_b[0].reshape(1, D_RNN), None),
        (w_rg_a[0].reshape(D_RNN, LANE), in_hbm),
        (w_rg_x[0].reshape(D_RNN, LANE), in_hbm),
        (b_ax, None),
        (lru_lambda[0].reshape(1, D_RNN), None),
        (w_rnn_out[0], in_hbm),
        (w_o[0], in_hbm),
        (norm_ffn[0].reshape(1, D), None),
        (w_ffn_in[0], in_hbm),
        (w_ffn_out[0], in_hbm),
        (norm_final.reshape(1, D), None),
    ]
    operands = [op for op, _ in operands_and_specs]
    in_specs = [spec if spec is not None else _resident(op.shape)
                for op, spec in operands_and_specs]
    n_carry_rows = len(POOL_SHIFTS) * PHASES
    d_in = w_in.shape[-1]
    assert d_in == O_GRNN + D_MODEL and d_in <= 2 * D_FF

    out = pl.pallas_call(
        functools.partial(_layer_kernel, nt, n_tiles),
        grid=(n_tiles + 1,),
        in_specs=in_specs,
        out_specs=pl.BlockSpec(memory_space=pl.ANY),
        out_shape=jax.ShapeDtypeStruct((B, S // PHASES, PHASES, D), x.dtype),
        scratch_shapes=[
            pltpu.VMEM((N_SLOTS, T, D), _F32),
            pltpu.VMEM((N_SLOTS, T, D), _F32),
            pltpu.SemaphoreType.DMA((N_SLOTS,)),
            pltpu.SemaphoreType.DMA((N_SLOTS,)),
            pltpu.VMEM((D, d_in), _BF16),
            pltpu.VMEM((D_POOL, LANE), _BF16),
            pltpu.VMEM((D_POOL, D), _BF16),
            pltpu.VMEM((D_RNN, 2 * LANE), _BF16),
            pltpu.VMEM((D_RNN, D), _BF16),
            pltpu.VMEM((D, D), _BF16),
            pltpu.VMEM((D, 2 * D_FF), _BF16),
            pltpu.VMEM((D_FF, D), _BF16),
            pltpu.VMEM((WIDE_STAGE_SLOTS, WIDE_STAGE_ROWS, 2 * D_FF), _F32),
            pltpu.VMEM((NARROW_STAGE_SLOTS, NARROW_STAGE_ROWS, D), _F32),
            pltpu.SemaphoreType.DMA((WIDE_STAGE_SLOTS,)),
            pltpu.SemaphoreType.DMA((NARROW_STAGE_SLOTS,)),
            pltpu.VMEM((T, D_POOL), _F32),
            pltpu.VMEM((T, D_RNN), _F32),
            pltpu.VMEM((T, D_RNN), _F32),
            pltpu.VMEM((T, D_RNN), _F32),
            pltpu.VMEM((T, D_RNN), _BF16),
            pltpu.VMEM((N_HEADS, T, LANE), _F32),
            pltpu.VMEM((N_HEADS, T, LANE), _F32),
            pltpu.VMEM((T, D_RNN), _BF16),
            pltpu.VMEM((T, D_FF), _BF16),
            pltpu.VMEM((T, D), _F32),
            pltpu.VMEM((T, D), _BF16),
            pltpu.VMEM((T, D), _BF16),
            pltpu.VMEM((T, D_POOL), _BF16),
            pltpu.VMEM((T, D_POOL), _BF16),
            pltpu.VMEM((T, D), _F32),
            pltpu.VMEM((T, D), _F32),
            pltpu.VMEM((T, D), _F32),
            pltpu.VMEM((n_carry_rows, D_POOL), _F32),
            pltpu.VMEM((PHASES, D_RNN), _F32),
            pltpu.VMEM((N_HEADS, LANE), _F32),
        ],
        compiler_params=pltpu.CompilerParams(
            dimension_semantics=("arbitrary",),
            vmem_limit_bytes=VMEM_LIMIT_BYTES,
        ),
        name="hybrid_layer",
    )(*operands)
    return out.reshape(B, S, D)
```
